```python
import math
import jax, jax.numpy as jnp
from jax import lax
import numpy as np

D_MODEL = 2048
BATCH = 1
SEQ = 16384
DEPTH = 1
DEC_BATCH = 4
DEC_SEQ = 2048
PAST_LEN = 128

GRID_W = 64
NA_HEADS = 16
NA_HEAD_DIM = 64
NA_DIM = NA_HEADS * NA_HEAD_DIM
NA_ROWS_MAX = 8
NA_KW = 16
SSD_HEADS = 16
SSD_HEAD_DIM = 64
SSD_DIM = SSD_HEADS * SSD_HEAD_DIM
SSD_GROUPS = 2
SSD_STATE = 128
SSD_CHUNK = 128
CONV_K = 5
CONV_DIM = SSD_DIM + 2 * SSD_GROUPS * SSD_STATE
HEADS_PER_GROUP = SSD_HEADS // SSD_GROUPS
IN_PROJ_DIM = 3 * NA_DIM + SSD_DIM + CONV_DIM + 2 * SSD_HEADS
MIX_DIM = NA_DIM + SSD_DIM
N_EXPERTS = 32
TOP_K = 4
D_FF = D_MODEL
SWIGLU_LIMIT = 7.0
SWIGLU_ALPHA = 1.702
EXPERT_BLOCK = 256
RMS_EPS = 1e-5

kernel_name = "hymba_natten_bissd_moe_encoder"


def rmsnorm(x, w):
    xf = x.astype(jnp.float32)
    xf = xf * lax.rsqrt(jnp.mean(xf * xf, axis=-1, keepdims=True) + RMS_EPS)
    return (xf * w.astype(jnp.float32)).astype(x.dtype)


def neighbourhood_attention(q, k, v, rpb):
    b, t, _ = q.shape
    rows = t // GRID_W
    kh = min(NA_ROWS_MAX, rows)
    qg = q.reshape(b, rows, GRID_W, NA_HEADS, NA_HEAD_DIM)
    kg = k.reshape(b, rows, GRID_W, NA_HEADS, NA_HEAD_DIM)
    vg = v.reshape(b, rows, GRID_W, NA_HEADS, NA_HEAD_DIM)
    cols = np.arange(GRID_W)
    c0 = np.clip(cols - NA_KW // 2, 0, GRID_W - NA_KW)
    cidx = c0[:, None] + np.arange(NA_KW)[None, :]
    col_off = cidx - cols[:, None] + (NA_KW - 1)
    rpb_cols = rpb[:, :, col_off]
    scale = NA_HEAD_DIM ** -0.5

    def attend_row(r):
        r0 = jnp.clip(r - kh // 2, 0, rows - kh)
        q_r = lax.dynamic_index_in_dim(qg, r, axis=1, keepdims=False)
        k_rows = lax.dynamic_slice_in_dim(kg, r0, kh, axis=1)
        v_rows = lax.dynamic_slice_in_dim(vg, r0, kh, axis=1)
        k_win = k_rows[:, :, cidx]
        v_win = v_rows[:, :, cidx]
        row_off = r0 + jnp.arange(kh) - r + (NA_ROWS_MAX - 1)
        bias = jnp.transpose(rpb_cols[:, row_off], (0, 2, 1, 3))
        s = jnp.einsum('bchd,bacwhd->bhcaw', q_r, k_win).astype(jnp.float32) * scale
        s = s + bias[None].astype(jnp.float32)
        p = jax.nn.softmax(s.reshape(b, NA_HEADS, GRID_W, kh * NA_KW), axis=-1)
        p = p.reshape(b, NA_HEADS, GRID_W, kh, NA_KW).astype(v.dtype)
        return jnp.einsum('bhcaw,bacwhd->bchd', p, v_win)

    out = lax.map(attend_row, jnp.arange(rows))
    return jnp.moveaxis(out, 0, 1).reshape(b, t, NA_DIM)


def centred_depthwise_conv(u, w, bias):
    c = u.shape[-1]
    out = lax.conv_general_dilated(
        u, w[:, None, :].astype(u.dtype), window_strides=(1,),
        padding=[(CONV_K // 2, CONV_K // 2)],
        dimension_numbers=('NWC', 'WIO', 'NWC'), feature_group_count=c)
    return out + bias.astype(u.dtype)


def ssd_scan(xh, dt, a_neg, bh, ch):
    b, l, h, p = xh.shape
    n = bh.shape[-1]
    nc = l // SSD_CHUNK
    xs = (xh * dt[..., None]).reshape(b, nc, SSD_CHUNK, h, p)
    bc = bh.reshape(b, nc, SSD_CHUNK, h, n)
    cc = ch.reshape(b, nc, SSD_CHUNK, h, n)
    a = jnp.transpose((dt * a_neg).reshape(b, nc, SSD_CHUNK, h), (0, 3, 1, 2))
    a_cum = jnp.cumsum(a, axis=-1)
    causal = jnp.tril(jnp.ones((SSD_CHUNK, SSD_CHUNK), dtype=bool))
    seg = a_cum[..., :, None] - a_cum[..., None, :]
    decay_in = jnp.exp(jnp.where(causal, seg, -jnp.inf))
    cb = jnp.einsum('bclhn,bcshn->bhcls', cc, bc)
    y_diag = jnp.einsum('bhcls,bcshp->bclhp', cb * decay_in, xs)
    decay_to_end = jnp.transpose(jnp.exp(a_cum[..., -1:] - a_cum), (0, 2, 3, 1))
    states = jnp.einsum('bclhn,bclhp->bchpn', bc * decay_to_end[..., None], xs)
    chunk_decay = jnp.exp(a_cum[..., -1])

    def step(carry, inp):
        st, dec = inp
        return carry * dec[..., None, None] + st, carry

    _, prev = lax.scan(step, jnp.zeros((b, h, p, n), jnp.float32),
                       (jnp.moveaxis(states, 1, 0), jnp.moveaxis(chunk_decay, 2, 0)))
    prev = jnp.moveaxis(prev, 0, 1)
    decay_from_start = jnp.transpose(jnp.exp(a_cum), (0, 2, 3, 1))
    y_off = jnp.einsum('bclhn,bchpn->bclhp', cc, prev) * decay_from_start[..., None]
    return (y_diag + y_off).reshape(b, l, h, p)


def bissd_mixer(z, xbc, dt_raw, conv_w, conv_b, dt_bias_fwd, dt_bias_bwd,
                a_log_fwd, a_log_bwd, d_skip, ssd_norm_w):
    b, l, _ = z.shape
    u = jax.nn.silu(centred_depthwise_conv(xbc, conv_w, conv_b)).astype(jnp.float32)
    xs, bs, cs = jnp.split(u, [SSD_DIM, SSD_DIM + SSD_GROUPS * SSD_STATE], axis=-1)
    xh = xs.reshape(b, l, SSD_HEADS, SSD_HEAD_DIM)
    bh = jnp.repeat(bs.reshape(b, l, SSD_GROUPS, SSD_STATE), HEADS_PER_GROUP, axis=2)
    chh = jnp.repeat(cs.reshape(b, l, SSD_GROUPS, SSD_STATE), HEADS_PER_GROUP, axis=2)
    dtf = dt_raw.astype(jnp.float32)
    dt_f = jax.nn.softplus(dtf[..., :SSD_HEADS] + dt_bias_fwd.astype(jnp.float32))
    dt_b = jax.nn.softplus(dtf[..., SSD_HEADS:] + dt_bias_bwd.astype(jnp.float32))
    a_f = -jnp.exp(a_log_fwd.astype(jnp.float32))
    a_b = -jnp.exp(a_log_bwd.astype(jnp.float32))
    y_f = ssd_scan(xh, dt_f, a_f, bh, chh)
    flip = lambda t: jnp.flip(t, axis=1)
    y_b = flip(ssd_scan(flip(xh), flip(dt_b), a_b, flip(bh), flip(chh)))
    y = y_f + y_b + d_skip.astype(jnp.float32)[None, None, :, None] * xh
    y = y.reshape(b, l, SSD_DIM) * jax.nn.silu(z.astype(jnp.float32))
    yg = y.reshape(b, l, SSD_GROUPS, SSD_DIM // SSD_GROUPS)
    yg = yg * lax.rsqrt(jnp.mean(yg * yg, axis=-1, keepdims=True) + RMS_EPS)
    y = yg.reshape(b, l, SSD_DIM) * ssd_norm_w.astype(jnp.float32)
    return y.astype(z.dtype)


def moe_ffn(h, router_w, router_b, w_gate, b_gate, w_up, b_up, w_down, b_down):
    t, d = h.shape
    m = t * TOP_K
    logits = (h @ router_w).astype(jnp.float32) + router_b.astype(jnp.float32)
    top_logit, top_e = lax.top_k(logits, TOP_K)
    gate_w = jax.nn.softmax(top_logit, axis=-1)
    flat_e = top_e.reshape(m)
    order = jnp.argsort(flat_e)
    e_sorted = flat_e[order]
    tok_sorted = (order // TOP_K).astype(jnp.int32)
    counts = jnp.bincount(flat_e, length=N_EXPERTS).astype(jnp.int32)
    padded = (counts + EXPERT_BLOCK - 1) // EXPERT_BLOCK * EXPERT_BLOCK
    pad_end = jnp.cumsum(padded)
    pad_start = pad_end - padded
    grp_start = jnp.cumsum(counts) - counts
    dest = pad_start[e_sorted] + jnp.arange(m, dtype=jnp.int32) - grp_start[e_sorted]
    n_blocks = -(-m // EXPERT_BLOCK) + N_EXPERTS
    slots = n_blocks * EXPERT_BLOCK
    slot_tok = jnp.full((slots,), t, jnp.int32).at[dest].set(tok_sorted)
    slot_w = jnp.zeros((slots,), jnp.float32).at[dest].set(gate_w.reshape(m)[order])
    block_e = jnp.minimum(jnp.searchsorted(pad_end, jnp.arange(n_blocks) * EXPERT_BLOCK, side='right'),
                          N_EXPERTS - 1)
    h_pad = jnp.concatenate([h, jnp.zeros((1, d), h.dtype)], axis=0)

    def expert_block(args):
        e, toks = args
        xb = h_pad[toks]
        g = jnp.minimum(xb @ w_gate[e] + b_gate[e], SWIGLU_LIMIT)
        u = jnp.clip(xb @ w_up[e] + b_up[e], -SWIGLU_LIMIT, SWIGLU_LIMIT)
        act = (u + 1.0) * (g * jax.nn.sigmoid(SWIGLU_ALPHA * g))
        return act @ w_down[e] + b_down[e]

    out = lax.map(expert_block, (block_e, slot_tok.reshape(n_blocks, EXPERT_BLOCK)))
    out = out.reshape(slots, d) * slot_w[:, None].astype(out.dtype)
    return jax.ops.segment_sum(out, slot_tok, num_segments=t + 1)[:t]


def encoder_layer(x, norm_mix_w, w_in, na_rpb, conv_w, conv_b, dt_bias_fwd, dt_bias_bwd,
                  a_log_fwd, a_log_bwd, d_skip, ssd_norm_w, w_out, norm_ffn_w,
                  router_w, router_b, w_gate, b_gate, w_up, b_up, w_down, b_down):
    b, t, d = x.shape
    h = rmsnorm(x, norm_mix_w)
    proj = h @ w_in
    splits = [NA_DIM, 2 * NA_DIM, 3 * NA_DIM, 3 * NA_DIM + SSD_DIM, 3 * NA_DIM + SSD_DIM + CONV_DIM]
    q, k, v, z, xbc, dt_raw = jnp.split(proj, splits, axis=-1)
    na_out = neighbourhood_attention(q, k, v, na_rpb)
    ssd_out = bissd_mixer(z, xbc, dt_raw, conv_w, conv_b, dt_bias_fwd, dt_bias_bwd,
                          a_log_fwd, a_log_bwd, d_skip, ssd_norm_w)
    x = x + jnp.concatenate([na_out, ssd_out], axis=-1) @ w_out
    h2 = rmsnorm(x, norm_ffn_w).reshape(b * t, d)
    x = x + moe_ffn(h2, router_w, router_b, w_gate, b_gate, w_up, b_up, w_down, b_down).reshape(b, t, d)
    return x


def setup_inputs(seed: int = 0) -> dict:
    key = jax.random.key(seed)
    ks = jax.random.split(key, 24)
    f32 = jnp.float32

    def nrm(k, shape, scale):
        return jax.random.normal(k, shape, f32) * scale

    def dt_bias(k):
        dt = jnp.exp(jax.random.uniform(k, (DEPTH, SSD_HEADS), f32, math.log(1e-3), math.log(1e-1)))
        return dt + jnp.log(-jnp.expm1(-dt))

    return {
        "x_prompt": nrm(ks[0], (BATCH, SEQ, D_MODEL), 1.0),
        "x_sample": nrm(ks[1], (DEC_BATCH, DEC_SEQ, D_MODEL), 1.0),
        "norm_mix_w": 1.0 + nrm(ks[2], (DEPTH, D_MODEL), 0.01),
        "w_in": nrm(ks[3], (DEPTH, D_MODEL, IN_PROJ_DIM), D_MODEL ** -0.5),
        "na_rpb": nrm(ks[4], (DEPTH, NA_HEADS, 2 * NA_ROWS_MAX - 1, 2 * NA_KW - 1), 0.02),
        "conv_w": nrm(ks[5], (DEPTH, CONV_K, CONV_DIM), CONV_K ** -0.5),
        "conv_b": nrm(ks[6], (DEPTH, CONV_DIM), 0.01),
        "dt_bias_fwd": dt_bias(ks[7]),
        "dt_bias_bwd": dt_bias(ks[8]),
        "a_log_fwd": jnp.log(jax.random.uniform(ks[9], (DEPTH, SSD_HEADS), f32, 1.0, 16.0)),
        "a_log_bwd": jnp.log(jax.random.uniform(ks[10], (DEPTH, SSD_HEADS), f32, 1.0, 16.0)),
        "d_skip": 1.0 + nrm(ks[11], (DEPTH, SSD_HEADS), 0.01),
        "ssd_norm_w": 1.0 + nrm(ks[12], (DEPTH, SSD_DIM), 0.01),
        "w_out": nrm(ks[13], (DEPTH, MIX_DIM, D_MODEL), MIX_DIM ** -0.5),
        "norm_ffn_w": 1.0 + nrm(ks[14], (DEPTH, D_MODEL), 0.01),
        "router_w": nrm(ks[15], (DEPTH, D_MODEL, N_EXPERTS), D_MODEL ** -0.5),
        "router_b": nrm(ks[16], (DEPTH, N_EXPERTS), 0.01),
        "w_gate": nrm(ks[17], (DEPTH, N_EXPERTS, D_MODEL, D_FF), D_MODEL ** -0.5),
        "b_gate": nrm(ks[18], (DEPTH, N_EXPERTS, D_FF), 0.01),
        "w_up": nrm(ks[19], (DEPTH, N_EXPERTS, D_MODEL, D_FF), D_MODEL ** -0.5),
        "b_up": nrm(ks[20], (DEPTH, N_EXPERTS, D_FF), 0.01),
        "w_down": nrm(ks[21], (DEPTH, N_EXPERTS, D_FF, D_MODEL), D_FF ** -0.5),
        "b_down": nrm(ks[22], (DEPTH, N_EXPERTS, D_MODEL), 0.01),
        "final_norm_w": 1.0 + nrm(ks[23], (D_MODEL,), 0.01),
    }


def reference(x_prompt, x_sample, norm_mix_w, w_in, na_rpb, conv_w, conv_b, dt_bias_fwd, dt_bias_bwd,
              a_log_fwd, a_log_bwd, d_skip, ssd_norm_w, w_out, norm_ffn_w, router_w, router_b,
              w_gate, b_gate, w_up, b_up, w_down, b_down, final_norm_w):
    def trunk(x):
        for i in range(DEPTH):
            x = encoder_layer(x, norm_mix_w[i], w_in[i], na_rpb[i], conv_w[i], conv_b[i],
                              dt_bias_fwd[i], dt_bias_bwd[i], a_log_fwd[i], a_log_bwd[i],
                              d_skip[i], ssd_norm_w[i], w_out[i], norm_ffn_w[i],
                              router_w[i], router_b[i], w_gate[i], b_gate[i],
                              w_up[i], b_up[i], w_down[i], b_down[i])
        return rmsnorm(x, final_norm_w)

    y_prompt = trunk(x_prompt)
    y_sample = trunk(x_sample)
    return (y_prompt, y_sample)
```

```python
import functools
import math

import jax
import jax.numpy as jnp
from jax import lax
from jax.experimental import pallas as pl
from jax.experimental.pallas import tpu as pltpu

F32 = jnp.float32
BF16 = jnp.bfloat16

GRID_W = 64
NA_HEADS = 16
NA_HEAD_DIM = 64
NA_DIM = NA_HEADS * NA_HEAD_DIM
NA_ROWS = 8
NA_KW = 16
SSD_HEADS = 16
SSD_HEAD_DIM = 64
SSD_DIM = SSD_HEADS * SSD_HEAD_DIM
SSD_GROUPS = 2
SSD_STATE = 128
CONV_K = 5
BC_DIM = 2 * SSD_GROUPS * SSD_STATE
N_EXPERTS = 32
TOP_K = 4
SWIGLU_LIMIT = 7.0
SWIGLU_ALPHA = 1.702
RMS_EPS = 1e-5
PROJ_DIM = 3 * NA_DIM + SSD_DIM + SSD_DIM + BC_DIM
LANES = 128
NEG_BIG = -1e30

VMEM_LIMIT = 56 * 1024 * 1024


def _split_hi_lo(x):
    hi = x.astype(BF16)
    lo = (x - hi.astype(F32)).astype(BF16)
    return hi, lo


def _dot(a, b):
    return jnp.dot(a, b, preferred_element_type=F32)


IN_TM = 256
IN_CH = 512


def _in_proj_kernel(xp_ref, xs_ref, nw_ref, w_ref, wdt_hi_ref, wdt_lo_ref,
                    proj_ref, dt_ref, hhi_scr, hlo_scr, *, n_prompt_blocks):
    i = pl.program_id(0)

    def norm_from(x_ref):
        x = x_ref[...]
        ms = jnp.mean(x * x, axis=-1, keepdims=True)
        h = x * lax.rsqrt(ms + RMS_EPS) * nw_ref[...]
        hi, lo = _split_hi_lo(h)
        hhi_scr[...] = hi
        hlo_scr[...] = lo

    @pl.when(i < n_prompt_blocks)
    def _():
        norm_from(xp_ref)

    @pl.when(i >= n_prompt_blocks)
    def _():
        norm_from(xs_ref)

    h_hi = hhi_scr[...]
    for c in range(0, PROJ_DIM, IN_CH):
        proj_ref[:, c:c + IN_CH] = _dot(h_hi, w_ref[:, c:c + IN_CH]).astype(BF16)
    wdt_hi = wdt_hi_ref[...]
    dt_ref[...] = (_dot(h_hi, wdt_hi) + _dot(hlo_scr[...], wdt_hi) + _dot(h_hi, wdt_lo_ref[...]))


def _two_source_specs(tm, d, n_prompt_blocks):
    last = n_prompt_blocks - 1
    return (pl.BlockSpec((tm, d), lambda i: (jnp.minimum(i, last), 0)),
            pl.BlockSpec((tm, d), lambda i: (jnp.maximum(i - n_prompt_blocks, 0), 0)))


def _resident(shape):
    return pl.BlockSpec(shape, lambda i: (0,) * len(shape), pipeline_mode=pl.Buffered(1))


def in_proj(xp, xs, norm_w, w_main, wdt_hi, wdt_lo):
    tp, d = xp.shape
    ts = xs.shape[0]
    t = tp + ts
    npb = tp // IN_TM
    xp_spec, xs_spec = _two_source_specs(IN_TM, d, npb)
    return pl.pallas_call(
        functools.partial(_in_proj_kernel, n_prompt_blocks=npb),
        out_shape=(jax.ShapeDtypeStruct((t, PROJ_DIM), BF16),
                   jax.ShapeDtypeStruct((t, LANES), F32)),
        grid=(t // IN_TM,),
        in_specs=[xp_spec, xs_spec, _resident((1, d)), _resident((d, PROJ_DIM)),
                  _resident((d, LANES)), _resident((d, LANES))],
        out_specs=(pl.BlockSpec((IN_TM, PROJ_DIM), lambda i: (i, 0)),
                   pl.BlockSpec((IN_TM, LANES), lambda i: (i, 0))),
        scratch_shapes=[pltpu.VMEM((IN_TM, d), BF16), pltpu.VMEM((IN_TM, d), BF16)],
        compiler_params=pltpu.CompilerParams(dimension_semantics=("arbitrary",),
                                             vmem_limit_bytes=VMEM_LIMIT),
        name="in_proj",
    )(xp, xs, norm_w, w_main, wdt_hi, wdt_lo)


def _seq_start_len(pos, geom):
    b1, s1, b2, s2 = geom
    t1 = b1 * s1
    in_p = pos < t1
    start = jnp.where(in_p, (pos // s1) * s1, t1 + ((pos - t1) // s2) * s2)
    length = jnp.where(in_p, s1, s2)
    return start, length


NA_WIN = NA_ROWS * GRID_W
NA_PAIRS = NA_HEADS // 2


def _na_row_geometry(i, geom):
    start, length = _seq_start_len(i * GRID_W, geom)
    rows = length // GRID_W
    r = i - start // GRID_W
    r0 = jnp.clip(r - NA_ROWS // 2, 0, rows - NA_ROWS)
    return start + r0 * GRID_W, r - r0


def _na_kernel(q_ref, k_ref, v_ref, bias_ref, o_ref):
    lane = lax.broadcasted_iota(jnp.int32, (GRID_W, LANES), 1)
    first = lane < NA_HEAD_DIM
    zero = jnp.zeros((GRID_W, LANES), BF16)
    for p in range(NA_PAIRS):
        cols = slice(p * LANES, (p + 1) * LANES)
        qs = q_ref[:, cols] * jnp.asarray(NA_HEAD_DIM ** -0.5, BF16)
        qm = jnp.concatenate([jnp.where(first, qs, zero), jnp.where(first, zero, qs)], axis=0)
        s = lax.dot_general(qm, k_ref[:, cols], (((1,), (1,)), ((), ())),
                            preferred_element_type=F32)
        s = s + bias_ref[0, p]
        m = jnp.max(s, axis=-1, keepdims=True)
        e = jnp.exp(s - m)
        l = jnp.sum(e, axis=-1, keepdims=True)
        o = _dot(e.astype(BF16), v_ref[:, cols]) / l
        o_ref[:, cols] = jnp.where(first, o[:GRID_W], o[GRID_W:]).astype(BF16)


def na_bias_table(rpb):
    c = jnp.arange(GRID_W)
    w = jnp.arange(GRID_W)
    c0 = jnp.clip(c - NA_KW // 2, 0, GRID_W - NA_KW)
    valid = (w[None, :] >= c0[:, None]) & (w[None, :] < c0[:, None] + NA_KW)
    col_off = jnp.clip(w[None, :] - c[:, None] + NA_KW - 1, 0, 2 * NA_KW - 2)
    row_off = jnp.arange(NA_ROWS)[None, :] - jnp.arange(NA_ROWS)[:, None] + NA_ROWS - 1
    tab = rpb[:, row_off[:, :, None, None], col_off[None, None, :, :]]
    tab = jnp.where(valid[None, None, None], tab.astype(F32), NEG_BIG)
    tab = jnp.transpose(tab, (1, 0, 3, 2, 4))
    return tab.reshape(NA_ROWS, NA_PAIRS, 2 * GRID_W, NA_WIN)


def neighbourhood_attention(proj, bias_tab, geom):
    t = proj.shape[0]
    n_rows = t // GRID_W

    def win_map(col_block):
        return lambda i: (pl.multiple_of(_na_row_geometry(i, geom)[0], GRID_W), col_block * NA_DIM)

    win_block = (pl.Element(NA_WIN), pl.Element(NA_DIM))

    return pl.pallas_call(
        _na_kernel,
        out_shape=jax.ShapeDtypeStruct((t, NA_DIM), BF16),
        grid=(n_rows,),
        in_specs=[pl.BlockSpec((GRID_W, NA_DIM), lambda i: (i, 0)),
                  pl.BlockSpec(win_block, win_map(1)),
                  pl.BlockSpec(win_block, win_map(2)),
                  pl.BlockSpec((1, NA_PAIRS, 2 * GRID_W, NA_WIN),
                               lambda i: (_na_row_geometry(i, geom)[1], 0, 0, 0))],
        out_specs=pl.BlockSpec((GRID_W, NA_DIM), lambda i: (i, 0)),
        compiler_params=pltpu.CompilerParams(dimension_semantics=("arbitrary",),
                                             vmem_limit_bytes=VMEM_LIMIT),
        name="na_attention",
    )(proj, proj, proj, bias_tab)


CONV_TL = 256
HALO = 16
X_COL_BLOCK = (3 * NA_DIM + SSD_DIM) // SSD_DIM
BC_COL_BLOCK = (3 * NA_DIM + 2 * SSD_DIM) // BC_DIM
CONV_CH = 512


def _conv_kernel(xm_ref, xp_ref, xn_ref, bm_ref, bp_ref, bn_ref, w_ref, b_ref,
                 ux_ref, ubc_ref, ext_scr, *, geom):
    tl = CONV_TL
    pos = pl.program_id(0) * tl
    start, length = _seq_start_len(pos, geom)
    has_prev = pos > start
    has_next = pos + tl < start + length
    zx = jnp.zeros((HALO, SSD_DIM), F32)
    zb = jnp.zeros((HALO, BC_DIM), F32)
    ext_scr[0:HALO, 0:SSD_DIM] = jnp.where(has_prev, xp_ref[...].astype(F32), zx)
    ext_scr[0:HALO, SSD_DIM:] = jnp.where(has_prev, bp_ref[...].astype(F32), zb)
    ext_scr[HALO:HALO + tl, 0:SSD_DIM] = xm_ref[...].astype(F32)
    ext_scr[HALO:HALO + tl, SSD_DIM:] = bm_ref[...].astype(F32)
    ext_scr[HALO + tl:, 0:SSD_DIM] = jnp.where(has_next, xn_ref[...].astype(F32), zx)
    ext_scr[HALO + tl:, SSD_DIM:] = jnp.where(has_next, bn_ref[...].astype(F32), zb)
    for c0 in range(0, SSD_DIM + BC_DIM, CONV_CH):
        cols = slice(c0, c0 + CONV_CH)
        acc = jnp.broadcast_to(b_ref[:, cols], (tl, CONV_CH))
        for k in range(CONV_K):
            r0 = HALO - CONV_K // 2 + k
            acc = acc + w_ref[k:k + 1, cols] * ext_scr[r0:r0 + tl, cols]
        u = (acc * jax.nn.sigmoid(acc)).astype(BF16)
        if c0 < SSD_DIM:
            ux_ref[:, cols] = u
        else:
            ubc_ref[:, c0 - SSD_DIM:c0 - SSD_DIM + CONV_CH] = u


def conv_silu(proj, conv_w, conv_b, geom):
    t = proj.shape[0]
    tl = CONV_TL
    per = tl // HALO
    n_halo = t // HALO

    def prev_map(cb):
        return lambda i: (jnp.maximum(i * per - 1, 0), cb)

    def next_map(cb):
        return lambda i: (jnp.minimum((i + 1) * per, n_halo - 1), cb)

    c = SSD_DIM + BC_DIM
    return pl.pallas_call(
        functools.partial(_conv_kernel, geom=geom),
        out_shape=(jax.ShapeDtypeStruct((t, SSD_DIM), BF16), jax.ShapeDtypeStruct((t, BC_DIM), BF16)),
        grid=(t // tl,),
        in_specs=[pl.BlockSpec((tl, SSD_DIM), lambda i: (i, X_COL_BLOCK)),
                  pl.BlockSpec((HALO, SSD_DIM), prev_map(X_COL_BLOCK)),
                  pl.BlockSpec((HALO, SSD_DIM), next_map(X_COL_BLOCK)),
                  pl.BlockSpec((tl, BC_DIM), lambda i: (i, BC_COL_BLOCK)),
                  pl.BlockSpec((HALO, BC_DIM), prev_map(BC_COL_BLOCK)),
                  pl.BlockSpec((HALO, BC_DIM), next_map(BC_COL_BLOCK)),
                  pl.BlockSpec((8, c), lambda i: (0, 0)),
                  pl.BlockSpec((1, c), lambda i: (0, 0))],
        out_specs=(pl.BlockSpec((tl, SSD_DIM), lambda i: (i, 0)),
                   pl.BlockSpec((tl, BC_DIM), lambda i: (i, 0))),
        scratch_shapes=[pltpu.VMEM((tl + 2 * HALO, c), F32)],
        compiler_params=pltpu.CompilerParams(dimension_semantics=("arbitrary",),
                                             vmem_limit_bytes=VMEM_LIMIT),
        name="conv_silu",
    )(proj, proj, proj, proj, proj, proj, conv_w, conv_b)


SSD_L = 128
GROUP_DIM = SSD_DIM // SSD_GROUPS
HEADS_PER_GROUP = SSD_HEADS // SSD_GROUPS
BWD = SSD_HEADS


def _softplus(x):
    return jnp.maximum(x, 0.0) + jnp.log1p(jnp.exp(-jnp.abs(x)))


def _dt_and_a(dt_ref, dtbias_ref, alog_ref):
    dtv = _softplus(dt_ref[...] + dtbias_ref[...])
    return dtv, dtv * (-jnp.exp(alog_ref[...]))


def _tri_dot(tri, a):
    a_hi, a_lo = _split_hi_lo(a)
    return _dot(tri, a_hi) + _dot(tri, a_lo)


def _expand_heads(e, expand):
    e_hi, e_lo = _split_hi_lo(e)
    return _dot(e_hi, expand) + _dot(e_lo, expand)


def _cross_chunk(c_ref_cols, state_scr, g):
    sg = state_scr[g * GROUP_DIM:(g + 1) * GROUP_DIM, :].astype(BF16)
    return lax.dot_general(c_ref_cols, sg, (((1,), (1,)), ((), ())), preferred_element_type=F32)


def _update_state(state_scr, x_t, w_t, cd_t, b_cols, row_off):
    for g in range(SSD_GROUPS):
        parts = []
        for hh in range(HEADS_PER_GROUP):
            h = g * HEADS_PER_GROUP + hh
            parts.append((x_t[h * SSD_HEAD_DIM:(h + 1) * SSD_HEAD_DIM, :]
                          * w_t[row_off + h:row_off + h + 1, :]).astype(BF16))
        upd = _dot(jnp.concatenate(parts, axis=0), b_cols[g])
        for hh in range(HEADS_PER_GROUP):
            h = g * HEADS_PER_GROUP + hh
            rows = slice(h * SSD_HEAD_DIM, (h + 1) * SSD_HEAD_DIM)
            state_scr[rows, :] = (state_scr[rows, :] * cd_t[row_off + h:row_off + h + 1, 0:1]
                                  + upd[hh * SSD_HEAD_DIM:(hh + 1) * SSD_HEAD_DIM, :])


def _ssd_bwd_kernel(ux_ref, ubc_ref, dt_ref, dtbias_ref, alog_ref, expand_ref,
                    yoff_ref, state_scr, *, geom, n_chunks):
    pos = (n_chunks - 1 - pl.program_id(0)) * SSD_L
    start, length = _seq_start_len(pos, geom)

    @pl.when(pos + SSD_L == start + length)
    def _():
        state_scr[...] = jnp.zeros_like(state_scr)

    dtv, a = _dt_and_a(dt_ref, dtbias_ref, alog_ref)
    li = lax.broadcasted_iota(jnp.int32, (SSD_L, SSD_L), 0)
    si = lax.broadcasted_iota(jnp.int32, (SSD_L, SSD_L), 1)
    triu = jnp.where(si >= li, 1.0, 0.0).astype(BF16)
    rcum = _tri_dot(triu, a)
    e_full = _expand_heads(jnp.exp(rcum), expand_ref[...])
    n = SSD_STATE
    b_cols = [ubc_ref[:, g * n:(g + 1) * n] for g in range(SSD_GROUPS)]
    for g in range(SSD_GROUPS):
        c_g = ubc_ref[:, (SSD_GROUPS + g) * n:(SSD_GROUPS + g + 1) * n]
        cols = slice(g * GROUP_DIM, (g + 1) * GROUP_DIM)
        yoff_ref[:, cols] = _cross_chunk(c_g, state_scr, g) * e_full[:, cols]
    rcum_t = rcum.T
    w_t = jnp.exp(rcum_t[:, 0:1] - rcum_t) * dtv.T
    cd_t = jnp.exp(rcum_t[:, 0:1])
    _update_state(state_scr, ux_ref[...].astype(F32).T, w_t, cd_t, b_cols, BWD)


def _ssd_fwd_kernel(ux_ref, ubc_ref, dt_ref, z_ref, yoffb_ref, dtbias_ref, alog_ref, expand_ref,
                    dskip_ref, normw_ref, out_ref, state_scr, *, geom):
    pos = pl.program_id(0) * SSD_L
    start, _ = _seq_start_len(pos, geom)

    @pl.when(pos == start)
    def _():
        state_scr[...] = jnp.zeros_like(state_scr)

    dtv, a = _dt_and_a(dt_ref, dtbias_ref, alog_ref)
    li = lax.broadcasted_iota(jnp.int32, (SSD_L, SSD_L), 0)
    si = lax.broadcasted_iota(jnp.int32, (SSD_L, SSD_L), 1)
    tril = jnp.where(si <= li, 1.0, 0.0).astype(BF16)
    triu = jnp.where(si >= li, 1.0, 0.0).astype(BF16)
    cum = _tri_dot(tril, a)
    rcum = _tri_dot(triu, a)
    cum_t, rcum_t, dt_t = cum.T, rcum.T, dtv.T
    n = SSD_STATE
    b_cols = [ubc_ref[:, g * n:(g + 1) * n] for g in range(SSD_GROUPS)]
    c_cols = [ubc_ref[:, (SSD_GROUPS + g) * n:(SSD_GROUPS + g + 1) * n] for g in range(SSD_GROUPS)]
    gmat = [lax.dot_general(c_cols[g], b_cols[g], (((1,), (1,)), ((), ())),
                            preferred_element_type=F32) for g in range(SSD_GROUPS)]
    e_full = _expand_heads(jnp.exp(cum), expand_ref[...])
    cross = [_cross_chunk(c_cols[g], state_scr, g) for g in range(SSD_GROUPS)]
    lane = lax.broadcasted_iota(jnp.int32, (SSD_L, LANES), 1)
    lower = li > si
    upper = li < si
    gated = []
    for p in range(SSD_HEADS // 2):
        cols = slice(p * LANES, (p + 1) * LANES)
        mats = []
        for par in range(2):
            h = 2 * p + par
            g = h // HEADS_PER_GROUP
            seg = jnp.where(li >= si, cum[:, h:h + 1] - cum_t[h:h + 1, :],
                            rcum[:, BWD + h:BWD + h + 1] - rcum_t[BWD + h:BWD + h + 1, :])
            dt_f = dt_t[h:h + 1, :]
            dt_b = dt_t[BWD + h:BWD + h + 1, :]
            dsel = jnp.where(lower, dt_f, jnp.where(upper, dt_b, dt_f + dt_b))
            mats.append((gmat[g] * jnp.exp(seg) * dsel).astype(BF16))
        yp = _dot(jnp.concatenate(mats, axis=0), ux_ref[:, cols])
        y = jnp.where(lane < SSD_HEAD_DIM, yp[:SSD_L], yp[SSD_L:])
        g = (2 * p) // HEADS_PER_GROUP
        gc = slice(p * LANES - g * GROUP_DIM, (p + 1) * LANES - g * GROUP_DIM)
        y = y + cross[g][:, gc] * e_full[:, cols]
        y = y + yoffb_ref[:, cols] + dskip_ref[:, cols] * ux_ref[:, cols].astype(F32)
        zf = z_ref[:, cols].astype(F32)
        gated.append(y * (zf * jax.nn.sigmoid(zf)))
    per_group = len(gated) // SSD_GROUPS
    for g in range(SSD_GROUPS):
        mine = gated[g * per_group:(g + 1) * per_group]
        ssq = sum(jnp.sum(v * v, axis=-1, keepdims=True) for v in mine)
        scale = lax.rsqrt(ssq / GROUP_DIM + RMS_EPS)
        for j, v in enumerate(mine):
            cols = slice((g * per_group + j) * LANES, (g * per_group + j + 1) * LANES)
            out_ref[:, cols] = (v * scale * normw_ref[:, cols]).astype(BF16)
    w_t = jnp.exp(cum_t[:, SSD_L - 1:SSD_L] - cum_t) * dt_t
    cd_t = jnp.exp(cum_t[:, SSD_L - 1:SSD_L])
    _update_state(state_scr, ux_ref[...].astype(F32).T, w_t, cd_t, b_cols, 0)


def _head_expand_matrix(row_off):
    h = jnp.arange(LANES)[:, None]
    c = jnp.arange(SSD_DIM)[None, :] // SSD_HEAD_DIM
    return jnp.where(h == c + row_off, 1.0, 0.0).astype(BF16)


def bissd(ux, ubc, dt_raw, proj, dt_bias_row, a_log_row, d_skip_row, norm_w_row, geom):
    t = ux.shape[0]
    n_chunks = t // SSD_L
    cp = pltpu.CompilerParams(dimension_semantics=("arbitrary",), vmem_limit_bytes=VMEM_LIMIT)
    row128 = pl.BlockSpec((1, LANES), lambda i: (0, 0))
    row_dim = pl.BlockSpec((1, SSD_DIM), lambda i: (0, 0))
    expand_spec = pl.BlockSpec((LANES, SSD_DIM), lambda i: (0, 0))
    state = pltpu.VMEM((SSD_DIM, SSD_STATE), F32)

    def rev(width):
        return pl.BlockSpec((SSD_L, width), lambda i: (n_chunks - 1 - i, 0))

    def fwd(width, col_block=0):
        return pl.BlockSpec((SSD_L, width), lambda i: (i, col_block))

    yoffb = pl.pallas_call(
        functools.partial(_ssd_bwd_kernel, geom=geom, n_chunks=n_chunks),
        out_shape=jax.ShapeDtypeStruct((t, SSD_DIM), F32),
        grid=(n_chunks,),
        in_specs=[rev(SSD_DIM), rev(BC_DIM), rev(LANES), row128, row128, expand_spec],
        out_specs=rev(SSD_DIM),
        scratch_shapes=[state],
        compiler_params=cp,
        name="ssd_backward",
    )(ux, ubc, dt_raw, dt_bias_row, a_log_row, _head_expand_matrix(BWD))
    z_col_block = 3 * NA_DIM // SSD_DIM
    return pl.pallas_call(
        functools.partial(_ssd_fwd_kernel, geom=geom),
        out_shape=jax.ShapeDtypeStruct((t, SSD_DIM), BF16),
        grid=(n_chunks,),
        in_specs=[fwd(SSD_DIM), fwd(BC_DIM), fwd(LANES), fwd(SSD_DIM, z_col_block), fwd(SSD_DIM),
                  row128, row128, expand_spec, row_dim, row_dim],
        out_specs=fwd(SSD_DIM),
        scratch_shapes=[state],
        compiler_params=cp,
        name="ssd_forward",
    )(ux, ubc, dt_raw, proj, yoffb, dt_bias_row, a_log_row, _head_expand_matrix(0),
      d_skip_row, norm_w_row)


OUT_TM = 256


def _out_proj_kernel(na_ref, ssd_ref, xp_ref, xs_ref, wna_ref, wssd_ref, nw_ref,
                     rw_hi_ref, rw_lo_ref, rb_ref, x1_ref, h2_ref, logit_ref, *, n_prompt_blocks):
    i = pl.program_id(0)
    mix = _dot(na_ref[...], wna_ref[...]) + _dot(ssd_ref[...], wssd_ref[...])

    @pl.when(i < n_prompt_blocks)
    def _():
        x1_ref[...] = xp_ref[...] + mix

    @pl.when(i >= n_prompt_blocks)
    def _():
        x1_ref[...] = xs_ref[...] + mix

    x1 = x1_ref[...]
    ms = jnp.mean(x1 * x1, axis=-1, keepdims=True)
    h2 = x1 * lax.rsqrt(ms + RMS_EPS) * nw_ref[...]
    hi, lo = _split_hi_lo(h2)
    h2_ref[...] = hi
    rw_hi = rw_hi_ref[...]
    logit_ref[...] = _dot(hi, rw_hi) + _dot(lo, rw_hi) + _dot(hi, rw_lo_ref[...]) + rb_ref[...]


def out_proj(na, ssd, xp, xs, w_na, w_ssd, norm_w, rw_hi, rw_lo, rb):
    t = na.shape[0]
    d = xp.shape[1]
    tm = OUT_TM
    npb = xp.shape[0] // tm
    xp_spec, xs_spec = _two_source_specs(tm, d, npb)
    row = lambda w: pl.BlockSpec((tm, w), lambda i: (i, 0))
    return pl.pallas_call(
        functools.partial(_out_proj_kernel, n_prompt_blocks=npb),
        out_shape=(jax.ShapeDtypeStruct((t, d), F32), jax.ShapeDtypeStruct((t, d), BF16),
                   jax.ShapeDtypeStruct((t, LANES), F32)),
        grid=(t // tm,),
        in_specs=[row(NA_DIM), row(SSD_DIM), xp_spec, xs_spec,
                  _resident((NA_DIM, d)), _resident((SSD_DIM, d)), _resident((1, d)),
                  _resident((d, LANES)), _resident((d, LANES)), _resident((1, LANES))],
        out_specs=(row(d), row(d), row(LANES)),
        compiler_params=pltpu.CompilerParams(dimension_semantics=("arbitrary",),
                                             vmem_limit_bytes=VMEM_LIMIT),
        name="out_proj",
    )(na, ssd, xp, xs, w_na, w_ssd, norm_w, rw_hi, rw_lo, rb)


MOE_BLK = 1024
MOE_TF = 256


def _moe_kernel(be_ref, nused_ref, x_ref, wg_ref, bg_ref, wu_ref, bu_ref, wd_ref, bd_ref, sw_ref,
                y_ref, acc_scr, *, n_f):
    b = pl.program_id(0)
    f = pl.program_id(1)
    used = b < nused_ref[0]

    @pl.when(used)
    def _():
        x = x_ref[...]
        g = jnp.minimum(_dot(x, wg_ref[0].astype(BF16)) + bg_ref[0], SWIGLU_LIMIT)
        u = jnp.clip(_dot(x, wu_ref[0].astype(BF16)) + bu_ref[0], -SWIGLU_LIMIT, SWIGLU_LIMIT)
        act = (u + 1.0) * (g * jax.nn.sigmoid(SWIGLU_ALPHA * g))
        part = _dot(act.astype(BF16), wd_ref[0].astype(BF16))

        @pl.when(f == 0)
        def _():
            acc_scr[...] = part

        @pl.when(f > 0)
        def _():
            acc_scr[...] += part

        @pl.when(f == n_f - 1)
        def _():
            y_ref[...] = ((acc_scr[...] + bd_ref[0]) * sw_ref[...]).astype(BF16)

    @pl.when(jnp.logical_and(jnp.logical_not(used), f == n_f - 1))
    def _():
        y_ref[...] = jnp.zeros_like(y_ref)


def moe_experts(block_e, n_used, xs, w_gate, b_gate, w_up, b_up, w_down, b_down, slot_w):
    slots, d = xs.shape
    d_ff = w_gate.shape[2]
    n_blocks = slots // MOE_BLK
    n_f = d_ff // MOE_TF

    def f_idx(b, f, nu):
        return jnp.where(b < nu[0], f, 0)

    grid_spec = pltpu.PrefetchScalarGridSpec(
        num_scalar_prefetch=2,
        grid=(n_blocks, n_f),
        in_specs=[pl.BlockSpec((MOE_BLK, d), lambda b, f, be, nu: (b, 0)),
                  pl.BlockSpec((1, d, MOE_TF), lambda b, f, be, nu: (be[b], 0, f_idx(b, f, nu))),
                  pl.BlockSpec((1, 1, MOE_TF), lambda b, f, be, nu: (be[b], 0, f_idx(b, f, nu))),
                  pl.BlockSpec((1, d, MOE_TF), lambda b, f, be, nu: (be[b], 0, f_idx(b, f, nu))),
                  pl.BlockSpec((1, 1, MOE_TF), lambda b, f, be, nu: (be[b], 0, f_idx(b, f, nu))),
                  pl.BlockSpec((1, MOE_TF, d), lambda b, f, be, nu: (be[b], f_idx(b, f, nu), 0)),
                  pl.BlockSpec((1, 1, d), lambda b, f, be, nu: (be[b], 0, 0)),
                  pl.BlockSpec((MOE_BLK, 1), lambda b, f, be, nu: (b, 0))],
        out_specs=pl.BlockSpec((MOE_BLK, d), lambda b, f, be, nu: (b, 0)),
        scratch_shapes=[pltpu.VMEM((MOE_BLK, d), F32)],
    )
    return pl.pallas_call(
        functools.partial(_moe_kernel, n_f=n_f),
        out_shape=jax.ShapeDtypeStruct((slots, d), BF16),
        grid_spec=grid_spec,
        compiler_params=pltpu.CompilerParams(dimension_semantics=("arbitrary", "arbitrary"),
                                             vmem_limit_bytes=VMEM_LIMIT),
        name="moe_experts",
    )(block_e, n_used, xs, w_gate, b_gate, w_up, b_up, w_down, b_down, slot_w)


FIN_TM = 256


def _final_kernel(x1_ref, yg_ref, nw_ref, o_ref):
    d = x1_ref.shape[1]
    x = x1_ref[...]
    for k in range(TOP_K):
        x = x + yg_ref[:, k * d:(k + 1) * d].astype(F32)
    ms = jnp.mean(x * x, axis=-1, keepdims=True)
    o_ref[...] = x * lax.rsqrt(ms + RMS_EPS) * nw_ref[...]


def combine_and_norm(x1, yg, norm_w):
    t, d = x1.shape
    tm = FIN_TM
    return pl.pallas_call(
        _final_kernel,
        out_shape=jax.ShapeDtypeStruct((t, d), F32),
        grid=(t // tm,),
        in_specs=[pl.BlockSpec((tm, d), lambda i: (i, 0)),
                  pl.BlockSpec((tm, TOP_K * d), lambda i: (i, 0)),
                  pl.BlockSpec((1, d), lambda i: (0, 0))],
        out_specs=pl.BlockSpec((tm, d), lambda i: (i, 0)),
        compiler_params=pltpu.CompilerParams(dimension_semantics=("arbitrary",),
                                             vmem_limit_bytes=VMEM_LIMIT),
        name="combine_norm",
    )(x1, yg, norm_w)


def _route(logits):
    t = logits.shape[0]
    m = t * TOP_K
    top_logit, top_e = lax.top_k(logits, TOP_K)
    gate_w = jax.nn.softmax(top_logit, axis=-1)
    flat_e = top_e.reshape(m).astype(jnp.int32)
    order = jnp.argsort(flat_e)
    e_sorted = flat_e[order]
    counts = jnp.bincount(flat_e, length=N_EXPERTS).astype(jnp.int32)
    padded = (counts + MOE_BLK - 1) // MOE_BLK * MOE_BLK
    pad_end = jnp.cumsum(padded)
    pad_start = pad_end - padded
    grp_start = jnp.cumsum(counts) - counts
    dest = pad_start[e_sorted] + jnp.arange(m, dtype=jnp.int32) - grp_start[e_sorted]
    n_blocks = -(-m // MOE_BLK) + N_EXPERTS
    slots = n_blocks * MOE_BLK
    slot_tok = jnp.zeros((slots,), jnp.int32).at[dest].set((order // TOP_K).astype(jnp.int32))
    slot_w = jnp.zeros((slots,), F32).at[dest].set(gate_w.reshape(m)[order])
    block_e = jnp.minimum(jnp.searchsorted(pad_end, jnp.arange(n_blocks, dtype=jnp.int32) * MOE_BLK,
                                           side='right'), N_EXPERTS - 1).astype(jnp.int32)
    n_used = (pad_end[-1:] // MOE_BLK).astype(jnp.int32)
    pos = jnp.zeros((m,), jnp.int32).at[order].set(dest)
    return slot_tok, slot_w, block_e, n_used, pos


def kernel(x_prompt, x_sample, norm_mix_w, w_in, na_rpb, conv_w, conv_b, dt_bias_fwd, dt_bias_bwd,
           a_log_fwd, a_log_bwd, d_skip, ssd_norm_w, w_out, norm_ffn_w, router_w, router_b,
           w_gate, b_gate, w_up, b_up, w_down, b_down, final_norm_w):
    b1, s1, d = x_prompt.shape
    b2, s2, _ = x_sample.shape
    geom = (b1, s1, b2, s2)
    for s in (s1, s2):
        assert s % (NA_ROWS * GRID_W) == 0 and s % CONV_TL == 0 and s % SSD_L == 0
    assert w_in.shape[0] == 1, "one layer"
    xp = x_prompt.reshape(b1 * s1, d)
    xs = x_sample.reshape(b2 * s2, d)
    t = xp.shape[0] + xs.shape[0]

    def pad_lanes(a):
        return jnp.pad(a, ((0, 0), (0, LANES - a.shape[1])))

    wi = w_in[0]
    wdt_hi, wdt_lo = _split_hi_lo(pad_lanes(wi[:, PROJ_DIM:]))
    proj, dt_raw = in_proj(xp, xs, norm_mix_w, wi[:, :PROJ_DIM].astype(BF16), wdt_hi, wdt_lo)

    na = neighbourhood_attention(proj, na_bias_table(na_rpb[0]), geom)

    ux, ubc = conv_silu(proj, jnp.pad(conv_w[0], ((0, 8 - CONV_K), (0, 0))), conv_b, geom)
    dt_bias_row = pad_lanes(jnp.concatenate([dt_bias_fwd, dt_bias_bwd], axis=1))
    a_log_row = pad_lanes(jnp.concatenate([a_log_fwd, a_log_bwd], axis=1))
    d_skip_row = jnp.repeat(d_skip, SSD_HEAD_DIM, axis=1)
    ssd = bissd(ux, ubc, dt_raw, proj, dt_bias_row, a_log_row, d_skip_row, ssd_norm_w, geom)

    wo = w_out[0].astype(BF16)
    rw_hi, rw_lo = _split_hi_lo(pad_lanes(router_w[0]))
    x1, h2, logits = out_proj(na, ssd, xp, xs, wo[:NA_DIM], wo[NA_DIM:], norm_ffn_w,
                              rw_hi, rw_lo, pad_lanes(router_b))

    slot_tok, slot_w, block_e, n_used, pos = _route(logits[:, :N_EXPERTS])
    y = moe_experts(block_e, n_used, h2[slot_tok], w_gate[0], b_gate[0][:, None, :],
                    w_up[0], b_up[0][:, None, :], w_down[0], b_down[0][:, None, :], slot_w[:, None])
    out = combine_and_norm(x1, y[pos].reshape(t, TOP_K * d), final_norm_w[None, :])
    return (out[:b1 * s1].reshape(b1, s1, d), out[b1 * s1:].reshape(b2, s2, d))
```

```python
import functools
import math

import jax
import jax.numpy as jnp
from jax import lax
from jax.experimental import pallas as pl
from jax.experimental.pallas import tpu as pltpu

F32 = jnp.float32
BF16 = jnp.bfloat16

GRID_W = 64
NA_HEADS = 16
NA_HEAD_DIM = 64
NA_DIM = NA_HEADS * NA_HEAD_DIM
NA_ROWS = 8
NA_KW = 16
SSD_HEADS = 16
SSD_HEAD_DIM = 64
SSD_DIM = SSD_HEADS * SSD_HEAD_DIM
SSD_GROUPS = 2
SSD_STATE = 128
CONV_K = 5
BC_DIM = 2 * SSD_GROUPS * SSD_STATE
N_EXPERTS = 32
TOP_K = 4
SWIGLU_LIMIT = 7.0
SWIGLU_ALPHA = 1.702
RMS_EPS = 1e-5
PROJ_DIM = 3 * NA_DIM + SSD_DIM + SSD_DIM + BC_DIM
LANES = 128
NEG_BIG = -1e30

VMEM_LIMIT = 56 * 1024 * 1024


def _split_hi_lo(x):
    hi = x.astype(BF16)
    lo = (x - hi.astype(F32)).astype(BF16)
    return hi, lo


def _dot(a, b):
    return jnp.dot(a, b, preferred_element_type=F32)


IN_TM = 256
IN_CH = 512


def _in_proj_kernel(xp_ref, xs_ref, nw_ref, w_ref, wdt_hi_ref, wdt_lo_ref,
                    proj_ref, dt_ref, hhi_scr, hlo_scr, *, n_prompt_blocks):
    i = pl.program_id(0)

    def norm_from(x_ref):
        x = x_ref[...]
        ms = jnp.mean(x * x, axis=-1, keepdims=True)
        h = x * lax.rsqrt(ms + RMS_EPS) * nw_ref[...]
        hi, lo = _split_hi_lo(h)
        hhi_scr[...] = hi
        hlo_scr[...] = lo

    @pl.when(i < n_prompt_blocks)
    def _():
        norm_from(xp_ref)

    @pl.when(i >= n_prompt_blocks)
    def _():
        norm_from(xs_ref)

    h_hi = hhi_scr[...]
    for c in range(0, PROJ_DIM, IN_CH):
        proj_ref[:, c:c + IN_CH] = _dot(h_hi, w_ref[:, c:c + IN_CH]).astype(BF16)
    wdt_hi = wdt_hi_ref[...]
    dt_ref[...] = (_dot(h_hi, wdt_hi) + _dot(hlo_scr[...], wdt_hi) + _dot(h_hi, wdt_lo_ref[...]))


def _two_source_specs(tm, d, n_prompt_blocks):
    last = n_prompt_blocks - 1
    return (pl.BlockSpec((tm, d), lambda i: (jnp.minimum(i, last), 0)),
            pl.BlockSpec((tm, d), lambda i: (jnp.maximum(i - n_prompt_blocks, 0), 0)))


def _resident(shape):
    return pl.BlockSpec(shape, lambda i: (0,) * len(shape), pipeline_mode=pl.Buffered(1))


def in_proj(xp, xs, norm_w, w_main, wdt_hi, wdt_lo):
    tp, d = xp.shape
    ts = xs.shape[0]
    t = tp + ts
    npb = tp // IN_TM
    xp_spec, xs_spec = _two_source_specs(IN_TM, d, npb)
    return pl.pallas_call(
        functools.partial(_in_proj_kernel, n_prompt_blocks=npb),
        out_shape=(jax.ShapeDtypeStruct((t, PROJ_DIM), BF16),
                   jax.ShapeDtypeStruct((t, LANES), F32)),
        grid=(t // IN_TM,),
        in_specs=[xp_spec, xs_spec, _resident((1, d)), _resident((d, PROJ_DIM)),
                  _resident((d, LANES)), _resident((d, LANES))],
        out_specs=(pl.BlockSpec((IN_TM, PROJ_DIM), lambda i: (i, 0)),
                   pl.BlockSpec((IN_TM, LANES), lambda i: (i, 0))),
        scratch_shapes=[pltpu.VMEM((IN_TM, d), BF16), pltpu.VMEM((IN_TM, d), BF16)],
        compiler_params=pltpu.CompilerParams(dimension_semantics=("arbitrary",),
                                             vmem_limit_bytes=VMEM_LIMIT),
        name="in_proj",
    )(xp, xs, norm_w, w_main, wdt_hi, wdt_lo)


def _seq_start_len(pos, geom):
    b1, s1, b2, s2 = geom
    t1 = b1 * s1
    in_p = pos < t1
    start = jnp.where(in_p, (pos // s1) * s1, t1 + ((pos - t1) // s2) * s2)
    length = jnp.where(in_p, s1, s2)
    return start, length


NA_WIN = NA_ROWS * GRID_W
NA_PAIRS = NA_HEADS // 2


def _na_row_geometry(i, geom):
    start, length = _seq_start_len(i * GRID_W, geom)
    rows = length // GRID_W
    r = i - start // GRID_W
    r0 = jnp.clip(r - NA_ROWS // 2, 0, rows - NA_ROWS)
    return start + r0 * GRID_W, r - r0


def _na_kernel(q_ref, k_ref, v_ref, bias_ref, o_ref):
    lane = lax.broadcasted_iota(jnp.int32, (GRID_W, LANES), 1)
    first = lane < NA_HEAD_DIM
    zero = jnp.zeros((GRID_W, LANES), BF16)
    for p in range(NA_PAIRS):
        cols = slice(p * LANES, (p + 1) * LANES)
        qs = q_ref[:, cols] * jnp.asarray(NA_HEAD_DIM ** -0.5, BF16)
        qm = jnp.concatenate([jnp.where(first, qs, zero), jnp.where(first, zero, qs)], axis=0)
        s = lax.dot_general(qm, k_ref[:, cols], (((1,), (1,)), ((), ())),
                            preferred_element_type=F32)
        s = s + bias_ref[0, p]
        m = jnp.max(s, axis=-1, keepdims=True)
        e = jnp.exp(s - m)
        l = jnp.sum(e, axis=-1, keepdims=True)
        o = _dot(e.astype(BF16), v_ref[:, cols]) / l
        o_ref[:, cols] = jnp.where(first, o[:GRID_W], o[GRID_W:]).astype(BF16)


def na_bias_table(rpb):
    c = jnp.arange(GRID_W)
    w = jnp.arange(GRID_W)
    c0 = jnp.clip(c - NA_KW // 2, 0, GRID_W - NA_KW)
    valid = (w[None, :] >= c0[:, None]) & (w[None, :] < c0[:, None] + NA_KW)
    col_off = jnp.clip(w[None, :] - c[:, None] + NA_KW - 1, 0, 2 * NA_KW - 2)
    row_off = jnp.arange(NA_ROWS)[None, :] - jnp.arange(NA_ROWS)[:, None] + NA_ROWS - 1
    tab = rpb[:, row_off[:, :, None, None], col_off[None, None, :, :]]
    tab = jnp.where(valid[None, None, None], tab.astype(F32), NEG_BIG)
    tab = jnp.transpose(tab, (1, 0, 3, 2, 4))
    return tab.reshape(NA_ROWS, NA_PAIRS, 2 * GRID_W, NA_WIN)


def neighbourhood_attention(proj, bias_tab, geom):
    t = proj.shape[0]
    n_rows = t // GRID_W

    def win_map(col_block):
        return lambda i: (pl.multiple_of(_na_row_geometry(i, geom)[0], GRID_W), col_block * NA_DIM)

    win_block = (pl.Element(NA_WIN), pl.Element(NA_DIM))

    return pl.pallas_call(
        _na_kernel,
        out_shape=jax.ShapeDtypeStruct((t, NA_DIM), BF16),
        grid=(n_rows,),
        in_specs=[pl.BlockSpec((GRID_W, NA_DIM), lambda i: (i, 0)),
                  pl.BlockSpec(win_block, win_map(1)),
                  pl.BlockSpec(win_block, win_map(2)),
                  pl.BlockSpec((1, NA_PAIRS, 2 * GRID_W, NA_WIN),
                               lambda i: (_na_row_geometry(i, geom)[1], 0, 0, 0))],
        out_specs=pl.BlockSpec((GRID_W, NA_DIM), lambda i: (i, 0)),
        compiler_params=pltpu.CompilerParams(dimension_semantics=("arbitrary",),
                                             vmem_limit_bytes=VMEM_LIMIT),
        name="na_attention",
    )(proj, proj, proj, bias_tab)


CONV_TL = 256
HALO = 16
X_COL_BLOCK = (3 * NA_DIM + SSD_DIM) // SSD_DIM
BC_COL_BLOCK = (3 * NA_DIM + 2 * SSD_DIM) // BC_DIM
CONV_CH = 512


def _conv_kernel(xm_ref, xp_ref, xn_ref, bm_ref, bp_ref, bn_ref, w_ref, b_ref,
                 ux_ref, ubc_ref, ext_scr, *, geom):
    tl = CONV_TL
    pos = pl.program_id(0) * tl
    start, length = _seq_start_len(pos, geom)
    has_prev = pos > start
    has_next = pos + tl < start + length
    zx = jnp.zeros((HALO, SSD_DIM), F32)
    zb = jnp.zeros((HALO, BC_DIM), F32)
    ext_scr[0:HALO, 0:SSD_DIM] = jnp.where(has_prev, xp_ref[...].astype(F32), zx)
    ext_scr[0:HALO, SSD_DIM:] = jnp.where(has_prev, bp_ref[...].astype(F32), zb)
    ext_scr[HALO:HALO + tl, 0:SSD_DIM] = xm_ref[...].astype(F32)
    ext_scr[HALO:HALO + tl, SSD_DIM:] = bm_ref[...].astype(F32)
    ext_scr[HALO + tl:, 0:SSD_DIM] = jnp.where(has_next, xn_ref[...].astype(F32), zx)
    ext_scr[HALO + tl:, SSD_DIM:] = jnp.where(has_next, bn_ref[...].astype(F32), zb)
    for c0 in range(0, SSD_DIM + BC_DIM, CONV_CH):
        cols = slice(c0, c0 + CONV_CH)
        acc = jnp.broadcast_to(b_ref[:, cols], (tl, CONV_CH))
        for k in range(CONV_K):
            r0 = HALO - CONV_K // 2 + k
            acc = acc + w_ref[k:k + 1, cols] * ext_scr[r0:r0 + tl, cols]
        u = (acc * jax.nn.sigmoid(acc)).astype(BF16)
        if c0 < SSD_DIM:
            ux_ref[:, cols] = u
        else:
            ubc_ref[:, c0 - SSD_DIM:c0 - SSD_DIM + CONV_CH] = u


def conv_silu(proj, conv_w, conv_b, geom):
    t = proj.shape[0]
    tl = CONV_TL
    per = tl // HALO
    n_halo = t // HALO

    def prev_map(cb):
        return lambda i: (jnp.maximum(i * per - 1, 0), cb)

    def next_map(cb):
        return lambda i: (jnp.minimum((i + 1) * per, n_halo - 1), cb)

    c = SSD_DIM + BC_DIM
    return pl.pallas_call(
        functools.partial(_conv_kernel, geom=geom),
        out_shape=(jax.ShapeDtypeStruct((t, SSD_DIM), BF16), jax.ShapeDtypeStruct((t, BC_DIM), BF16)),
        grid=(t // tl,),
        in_specs=[pl.BlockSpec((tl, SSD_DIM), lambda i: (i, X_COL_BLOCK)),
                  pl.BlockSpec((HALO, SSD_DIM), prev_map(X_COL_BLOCK)),
                  pl.BlockSpec((HALO, SSD_DIM), next_map(X_COL_BLOCK)),
                  pl.BlockSpec((tl, BC_DIM), lambda i: (i, BC_COL_BLOCK)),
                  pl.BlockSpec((HALO, BC_DIM), prev_map(BC_COL_BLOCK)),
                  pl.BlockSpec((HALO, BC_DIM), next_map(BC_COL_BLOCK)),
                  pl.BlockSpec((8, c), lambda i: (0, 0)),
                  pl.BlockSpec((1, c), lambda i: (0, 0))],
        out_specs=(pl.BlockSpec((tl, SSD_DIM), lambda i: (i, 0)),
                   pl.BlockSpec((tl, BC_DIM), lambda i: (i, 0))),
        scratch_shapes=[pltpu.VMEM((tl + 2 * HALO, c), F32)],
        compiler_params=pltpu.CompilerParams(dimension_semantics=("arbitrary",),
                                             vmem_limit_bytes=VMEM_LIMIT),
        name="conv_silu",
    )(proj, proj, proj, proj, proj, proj, conv_w, conv_b)


SSD_L = 128
GROUP_DIM = SSD_DIM // SSD_GROUPS
HEADS_PER_GROUP = SSD_HEADS // SSD_GROUPS
BWD = SSD_HEADS


def _softplus(x):
    return jnp.maximum(x, 0.0) + jnp.log1p(jnp.exp(-jnp.abs(x)))


def _dt_and_a(dt_ref, dtbias_ref, alog_ref):
    dtv = _softplus(dt_ref[...] + dtbias_ref[...])
    return dtv, dtv * (-jnp.exp(alog_ref[...]))


def _tri_dot(tri, a):
    a_hi, a_lo = _split_hi_lo(a)
    return _dot(tri, a_hi) + _dot(tri, a_lo)


def _expand_heads(e, expand):
    e_hi, e_lo = _split_hi_lo(e)
    return _dot(e_hi, expand) + _dot(e_lo, expand)


def _cross_chunk(c_ref_cols, state_scr, g):
    sg = state_scr[g * GROUP_DIM:(g + 1) * GROUP_DIM, :].astype(BF16)
    return lax.dot_general(c_ref_cols, sg, (((1,), (1,)), ((), ())), preferred_element_type=F32)


def _update_state(state_scr, x_t, w_t, cd_t, b_cols, row_off):
    for g in range(SSD_GROUPS):
        parts = []
        for hh in range(HEADS_PER_GROUP):
            h = g * HEADS_PER_GROUP + hh
            parts.append((x_t[h * SSD_HEAD_DIM:(h + 1) * SSD_HEAD_DIM, :]
                          * w_t[row_off + h:row_off + h + 1, :]).astype(BF16))
        upd = _dot(jnp.concatenate(parts, axis=0), b_cols[g])
        for hh in range(HEADS_PER_GROUP):
            h = g * HEADS_PER_GROUP + hh
            rows = slice(h * SSD_HEAD_DIM, (h + 1) * SSD_HEAD_DIM)
            state_scr[rows, :] = (state_scr[rows, :] * cd_t[row_off + h:row_off + h + 1, 0:1]
                                  + upd[hh * SSD_HEAD_DIM:(hh + 1) * SSD_HEAD_DIM, :])


def _ssd_bwd_kernel(ux_ref, ubc_ref, dt_ref, dtbias_ref, alog_ref, expand_ref,
                    yoff_ref, state_scr, *, geom, n_chunks):
    pos = (n_chunks - 1 - pl.program_id(0)) * SSD_L
    start, length = _seq_start_len(pos, geom)

    @pl.when(pos + SSD_L == start + length)
    def _():
        state_scr[...] = jnp.zeros_like(state_scr)

    dtv, a = _dt_and_a(dt_ref, dtbias_ref, alog_ref)
    li = lax.broadcasted_iota(jnp.int32, (SSD_L, SSD_L), 0)
    si = lax.broadcasted_iota(jnp.int32, (SSD_L, SSD_L), 1)
    triu = jnp.where(si >= li, 1.0, 0.0).astype(BF16)
    rcum = _tri_dot(triu, a)
    e_full = _expand_heads(jnp.exp(rcum), expand_ref[...])
    n = SSD_STATE
    b_cols = [ubc_ref[:, g * n:(g + 1) * n] for g in range(SSD_GROUPS)]
    for g in range(SSD_GROUPS):
        c_g = ubc_ref[:, (SSD_GROUPS + g) * n:(SSD_GROUPS + g + 1) * n]
        cols = slice(g * GROUP_DIM, (g + 1) * GROUP_DIM)
        yoff_ref[:, cols] = _cross_chunk(c_g, state_scr, g) * e_full[:, cols]
    rcum_t = rcum.T
    w_t = jnp.exp(rcum_t[:, 0:1] - rcum_t) * dtv.T
    cd_t = jnp.exp(rcum_t[:, 0:1])
    _update_state(state_scr, ux_ref[...].astype(F32).T, w_t, cd_t, b_cols, BWD)


def _ssd_fwd_kernel(ux_ref, ubc_ref, dt_ref, z_ref, yoffb_ref, dtbias_ref, alog_ref, expand_ref,
                    dskip_ref, normw_ref, out_ref, state_scr, *, geom):
    pos = pl.program_id(0) * SSD_L
    start, _ = _seq_start_len(pos, geom)

    @pl.when(pos == start)
    def _():
        state_scr[...] = jnp.zeros_like(state_scr)

    dtv, a = _dt_and_a(dt_ref, dtbias_ref, alog_ref)
    li = lax.broadcasted_iota(jnp.int32, (SSD_L, SSD_L), 0)
    si = lax.broadcasted_iota(jnp.int32, (SSD_L, SSD_L), 1)
    tril = jnp.where(si <= li, 1.0, 0.0).astype(BF16)
    triu = jnp.where(si >= li, 1.0, 0.0).astype(BF16)
    cum = _tri_dot(tril, a)
    rcum = _tri_dot(triu, a)
    cum_t, rcum_t, dt_t = cum.T, rcum.T, dtv.T
    n = SSD_STATE
    b_cols = [ubc_ref[:, g * n:(g + 1) * n] for g in range(SSD_GROUPS)]
    c_cols = [ubc_ref[:, (SSD_GROUPS + g) * n:(SSD_GROUPS + g + 1) * n] for g in range(SSD_GROUPS)]
    gmat = [lax.dot_general(c_cols[g], b_cols[g], (((1,), (1,)), ((), ())),
                            preferred_element_type=F32) for g in range(SSD_GROUPS)]
    e_full = _expand_heads(jnp.exp(cum), expand_ref[...])
    cross = [_cross_chunk(c_cols[g], state_scr, g) for g in range(SSD_GROUPS)]
    lane = lax.broadcasted_iota(jnp.int32, (SSD_L, LANES), 1)
    lower = li > si
    upper = li < si
    gated = []
    for p in range(SSD_HEADS // 2):
        cols = slice(p * LANES, (p + 1) * LANES)
        mats = []
        for par in range(2):
            h = 2 * p + par
            g = h // HEADS_PER_GROUP
            seg = jnp.where(li >= si, cum[:, h:h + 1] - cum_t[h:h + 1, :],
                            rcum[:, BWD + h:BWD + h + 1] - rcum_t[BWD + h:BWD + h + 1, :])
            dt_f = dt_t[h:h + 1, :]
            dt_b = dt_t[BWD + h:BWD + h + 1, :]
            dsel = jnp.where(lower, dt_f, jnp.where(upper, dt_b, dt_f + dt_b))
            mats.append((gmat[g] * jnp.exp(seg) * dsel).astype(BF16))
        yp = _dot(jnp.concatenate(mats, axis=0), ux_ref[:, cols])
        y = jnp.where(lane < SSD_HEAD_DIM, yp[:SSD_L], yp[SSD_L:])
        g = (2 * p) // HEADS_PER_GROUP
        gc = slice(p * LANES - g * GROUP_DIM, (p + 1) * LANES - g * GROUP_DIM)
        y = y + cross[g][:, gc] * e_full[:, cols]
        y = y + yoffb_ref[:, cols] + dskip_ref[:, cols] * ux_ref[:, cols].astype(F32)
        zf = z_ref[:, cols].astype(F32)
        gated.append(y * (zf * jax.nn.sigmoid(zf)))
    per_group = len(gated) // SSD_GROUPS
    for g in range(SSD_GROUPS):
        mine = gated[g * per_group:(g + 1) * per_group]
        ssq = sum(jnp.sum(v * v, axis=-1, keepdims=True) for v in mine)
        scale = lax.rsqrt(ssq / GROUP_DIM + RMS_EPS)
        for j, v in enumerate(mine):
            cols = slice((g * per_group + j) * LANES, (g * per_group + j + 1) * LANES)
            out_ref[:, cols] = (v * scale * normw_ref[:, cols]).astype(BF16)
    w_t = jnp.exp(cum_t[:, SSD_L - 1:SSD_L] - cum_t) * dt_t
    cd_t = jnp.exp(cum_t[:, SSD_L - 1:SSD_L])
    _update_state(state_scr, ux_ref[...].astype(F32).T, w_t, cd_t, b_cols, 0)


def _head_expand_matrix(row_off):
    h = jnp.arange(LANES)[:, None]
    c = jnp.arange(SSD_DIM)[None, :] // SSD_HEAD_DIM
    return jnp.where(h == c + row_off, 1.0, 0.0).astype(BF16)


def bissd(ux, ubc, dt_raw, proj, dt_bias_row, a_log_row, d_skip_row, norm_w_row, geom):
    t = ux.shape[0]
    n_chunks = t // SSD_L
    cp = pltpu.CompilerParams(dimension_semantics=("arbitrary",), vmem_limit_bytes=VMEM_LIMIT)
    row128 = pl.BlockSpec((1, LANES), lambda i: (0, 0))
    row_dim = pl.BlockSpec((1, SSD_DIM), lambda i: (0, 0))
    expand_spec = pl.BlockSpec((LANES, SSD_DIM), lambda i: (0, 0))
    state = pltpu.VMEM((SSD_DIM, SSD_STATE), F32)

    def rev(width):
        return pl.BlockSpec((SSD_L, width), lambda i: (n_chunks - 1 - i, 0))

    def fwd(width, col_block=0):
        return pl.BlockSpec((SSD_L, width), lambda i: (i, col_block))

    yoffb = pl.pallas_call(
        functools.partial(_ssd_bwd_kernel, geom=geom, n_chunks=n_chunks),
        out_shape=jax.ShapeDtypeStruct((t, SSD_DIM), F32),
        grid=(n_chunks,),
        in_specs=[rev(SSD_DIM), rev(BC_DIM), rev(LANES), row128, row128, expand_spec],
        out_specs=rev(SSD_DIM),
        scratch_shapes=[state],
        compiler_params=cp,
        name="ssd_backward",
    )(ux, ubc, dt_raw, dt_bias_row, a_log_row, _head_expand_matrix(BWD))
    z_col_block = 3 * NA_DIM // SSD_DIM
    return pl.pallas_call(
        functools.partial(_ssd_fwd_kernel, geom=geom),
        out_shape=jax.ShapeDtypeStruct((t, SSD_DIM), BF16),
        grid=(n_chunks,),
        in_specs=[fwd(SSD_DIM), fwd(BC_DIM), fwd(LANES), fwd(SSD_DIM, z_col_block), fwd(SSD_DIM),
                  row128, row128, expand_spec, row_dim, row_dim],
        out_specs=fwd(SSD_DIM),
        scratch_shapes=[state],
        compiler_params=cp,
        name="ssd_forward",
    )(ux, ubc, dt_raw, proj, yoffb, dt_bias_row, a_log_row, _head_expand_matrix(0),
      d_skip_row, norm_w_row)


OUT_TM = 256


def _out_proj_kernel(na_ref, ssd_ref, xp_ref, xs_ref, wna_ref, wssd_ref, nw_ref,
                     rw_hi_ref, rw_lo_ref, rb_ref, x1_ref, h2_ref, logit_ref, *, n_prompt_blocks):
    i = pl.program_id(0)
    mix = _dot(na_ref[...], wna_ref[...]) + _dot(ssd_ref[...], wssd_ref[...])

    @pl.when(i < n_prompt_blocks)
    def _():
        x1_ref[...] = xp_ref[...] + mix

    @pl.when(i >= n_prompt_blocks)
    def _():
        x1_ref[...] = xs_ref[...] + mix

    x1 = x1_ref[...]
    ms = jnp.mean(x1 * x1, axis=-1, keepdims=True)
    h2 = x1 * lax.rsqrt(ms + RMS_EPS) * nw_ref[...]
    hi, lo = _split_hi_lo(h2)
    h2_ref[...] = hi
    rw_hi = rw_hi_ref[...]
    logit_ref[...] = _dot(hi, rw_hi) + _dot(lo, rw_hi) + _dot(hi, rw_lo_ref[...]) + rb_ref[...]


def out_proj(na, ssd, xp, xs, w_na, w_ssd, norm_w, rw_hi, rw_lo, rb):
    t = na.shape[0]
    d = xp.shape[1]
    tm = OUT_TM
    npb = xp.shape[0] // tm
    xp_spec, xs_spec = _two_source_specs(tm, d, npb)
    row = lambda w: pl.BlockSpec((tm, w), lambda i: (i, 0))
    return pl.pallas_call(
        functools.partial(_out_proj_kernel, n_prompt_blocks=npb),
        out_shape=(jax.ShapeDtypeStruct((t, d), F32), jax.ShapeDtypeStruct((t, d), BF16),
                   jax.ShapeDtypeStruct((t, LANES), F32)),
        grid=(t // tm,),
        in_specs=[row(NA_DIM), row(SSD_DIM), xp_spec, xs_spec,
                  _resident((NA_DIM, d)), _resident((SSD_DIM, d)), _resident((1, d)),
                  _resident((d, LANES)), _resident((d, LANES)), _resident((1, LANES))],
        out_specs=(row(d), row(d), row(LANES)),
        compiler_params=pltpu.CompilerParams(dimension_semantics=("arbitrary",),
                                             vmem_limit_bytes=VMEM_LIMIT),
        name="out_proj",
    )(na, ssd, xp, xs, w_na, w_ssd, norm_w, rw_hi, rw_lo, rb)


MOE_BLK = 1024
MOE_TF = 256


def _moe_kernel(be_ref, nused_ref, x_ref, wg_ref, bg_ref, wu_ref, bu_ref, wd_ref, bd_ref, sw_ref,
                y_ref, acc_scr, *, n_f):
    b = pl.program_id(0)
    f = pl.program_id(1)
    used = b < nused_ref[0]

    @pl.when(used)
    def _():
        x = x_ref[...]
        g = jnp.minimum(_dot(x, wg_ref[0].astype(BF16)) + bg_ref[0], SWIGLU_LIMIT)
        u = jnp.clip(_dot(x, wu_ref[0].astype(BF16)) + bu_ref[0], -SWIGLU_LIMIT, SWIGLU_LIMIT)
        act = (u + 1.0) * (g * jax.nn.sigmoid(SWIGLU_ALPHA * g))
        part = _dot(act.astype(BF16), wd_ref[0].astype(BF16))

        @pl.when(f == 0)
        def _():
            acc_scr[...] = part

        @pl.when(f > 0)
        def _():
            acc_scr[...] += part

        @pl.when(f == n_f - 1)
        def _():
            y_ref[...] = ((acc_scr[...] + bd_ref[0]) * sw_ref[...]).astype(BF16)

    @pl.when(jnp.logical_and(jnp.logical_not(used), f == n_f - 1))
    def _():
        y_ref[...] = jnp.zeros_like(y_ref)


def moe_experts(block_e, n_used, xs, w_gate, b_gate, w_up, b_up, w_down, b_down, slot_w):
    slots, d = xs.shape
    d_ff = w_gate.shape[2]
    n_blocks = slots // MOE_BLK
    n_f = d_ff // MOE_TF

    def f_idx(b, f, nu):
        return jnp.where(b < nu[0], f, 0)

    grid_spec = pltpu.PrefetchScalarGridSpec(
        num_scalar_prefetch=2,
        grid=(n_blocks, n_f),
        in_specs=[pl.BlockSpec((MOE_BLK, d), lambda b, f, be, nu: (b, 0)),
                  pl.BlockSpec((1, d, MOE_TF), lambda b, f, be, nu: (be[b], 0, f_idx(b, f, nu))),
                  pl.BlockSpec((1, 1, MOE_TF), lambda b, f, be, nu: (be[b], 0, f_idx(b, f, nu))),
                  pl.BlockSpec((1, d, MOE_TF), lambda b, f, be, nu: (be[b], 0, f_idx(b, f, nu))),
                  pl.BlockSpec((1, 1, MOE_TF), lambda b, f, be, nu: (be[b], 0, f_idx(b, f, nu))),
                  pl.BlockSpec((1, MOE_TF, d), lambda b, f, be, nu: (be[b], f_idx(b, f, nu), 0)),
                  pl.BlockSpec((1, 1, d), lambda b, f, be, nu: (be[b], 0, 0)),
                  pl.BlockSpec((MOE_BLK, 1), lambda b, f, be, nu: (b, 0))],
        out_specs=pl.BlockSpec((MOE_BLK, d), lambda b, f, be, nu: (b, 0)),
        scratch_shapes=[pltpu.VMEM((MOE_BLK, d), F32)],
    )
    return pl.pallas_call(
        functools.partial(_moe_kernel, n_f=n_f),
        out_shape=jax.ShapeDtypeStruct((slots, d), BF16),
        grid_spec=grid_spec,
        compiler_params=pltpu.CompilerParams(dimension_semantics=("arbitrary", "arbitrary"),
                                             vmem_limit_bytes=VMEM_LIMIT),
        name="moe_experts",
    )(block_e, n_used, xs, w_gate, b_gate, w_up, b_up, w_down, b_down, slot_w)


FIN_TM = 256


def _final_kernel(x1_ref, *refs):
    y_refs, nw_ref, o_ref = refs[:TOP_K], refs[TOP_K], refs[TOP_K + 1]
    x = x1_ref[...]
    for y_ref in y_refs:
        x = x + y_ref[...].astype(F32)
    ms = jnp.mean(x * x, axis=-1, keepdims=True)
    o_ref[...] = x * lax.rsqrt(ms + RMS_EPS) * nw_ref[...]


def combine_and_norm(x1, yg, norm_w):
    t, d = x1.shape
    tm = FIN_TM
    per_k = t // tm

    def choice(k):
        return pl.BlockSpec((tm, d), lambda i: (k * per_k + i, 0))

    return pl.pallas_call(
        _final_kernel,
        out_shape=jax.ShapeDtypeStruct((t, d), F32),
        grid=(per_k,),
        in_specs=[pl.BlockSpec((tm, d), lambda i: (i, 0))] + [choice(k) for k in range(TOP_K)]
                 + [pl.BlockSpec((1, d), lambda i: (0, 0))],
        out_specs=pl.BlockSpec((tm, d), lambda i: (i, 0)),
        compiler_params=pltpu.CompilerParams(dimension_semantics=("arbitrary",),
                                             vmem_limit_bytes=VMEM_LIMIT),
        name="combine_norm",
    )(x1, *([yg] * TOP_K), norm_w)


def _route(logits):
    t = logits.shape[0]
    m = t * TOP_K
    top_logit, top_e = lax.top_k(logits, TOP_K)
    gate_w = jax.nn.softmax(top_logit, axis=-1)
    flat_e = top_e.reshape(m).astype(jnp.int32)
    iota = jnp.arange(m, dtype=jnp.int32)
    e_sorted, order, w_sorted = lax.sort((flat_e, iota, gate_w.reshape(m)), num_keys=1, is_stable=True)
    counts = jnp.bincount(flat_e, length=N_EXPERTS).astype(jnp.int32)
    padded = (counts + MOE_BLK - 1) // MOE_BLK * MOE_BLK
    pad_end = jnp.cumsum(padded)
    pad_start = pad_end - padded
    grp_start = jnp.cumsum(counts) - counts
    dest = pad_start[e_sorted] + iota - grp_start[e_sorted]
    n_blocks = -(-m // MOE_BLK) + N_EXPERTS
    block_e = jnp.minimum(jnp.searchsorted(pad_end, jnp.arange(n_blocks, dtype=jnp.int32) * MOE_BLK,
                                           side='right'), N_EXPERTS - 1).astype(jnp.int32)
    n_used = (pad_end[-1:] // MOE_BLK).astype(jnp.int32)
    e_slot = jnp.repeat(block_e, MOE_BLK)
    off = jnp.arange(n_blocks * MOE_BLK, dtype=jnp.int32) - pad_start[e_slot]
    valid = off < counts[e_slot]
    src = jnp.clip(grp_start[e_slot] + off, 0, m - 1)
    slot_tok = jnp.where(valid, order[src] // TOP_K, 0)
    slot_w = jnp.where(valid, w_sorted[src], 0.0)
    _, pos = lax.sort((order, dest), num_keys=1)
    pos = pos.reshape(t, TOP_K).T.reshape(m)
    return slot_tok, slot_w, block_e, n_used, pos


def kernel(x_prompt, x_sample, norm_mix_w, w_in, na_rpb, conv_w, conv_b, dt_bias_fwd, dt_bias_bwd,
           a_log_fwd, a_log_bwd, d_skip, ssd_norm_w, w_out, norm_ffn_w, router_w, router_b,
           w_gate, b_gate, w_up, b_up, w_down, b_down, final_norm_w):
    b1, s1, d = x_prompt.shape
    b2, s2, _ = x_sample.shape
    geom = (b1, s1, b2, s2)
    for s in (s1, s2):
        assert s % (NA_ROWS * GRID_W) == 0 and s % CONV_TL == 0 and s % SSD_L == 0
    assert w_in.shape[0] == 1, "one layer"
    xp = x_prompt.reshape(b1 * s1, d)
    xs = x_sample.reshape(b2 * s2, d)
    t = xp.shape[0] + xs.shape[0]

    def pad_lanes(a):
        return jnp.pad(a, ((0, 0), (0, LANES - a.shape[1])))

    wi = w_in[0]
    wdt_hi, wdt_lo = _split_hi_lo(pad_lanes(wi[:, PROJ_DIM:]))
    proj, dt_raw = in_proj(xp, xs, norm_mix_w, wi[:, :PROJ_DIM].astype(BF16), wdt_hi, wdt_lo)

    na = neighbourhood_attention(proj, na_bias_table(na_rpb[0]), geom)

    ux, ubc = conv_silu(proj, jnp.pad(conv_w[0], ((0, 8 - CONV_K), (0, 0))), conv_b, geom)
    dt_bias_row = pad_lanes(jnp.concatenate([dt_bias_fwd, dt_bias_bwd], axis=1))
    a_log_row = pad_lanes(jnp.concatenate([a_log_fwd, a_log_bwd], axis=1))
    d_skip_row = jnp.repeat(d_skip, SSD_HEAD_DIM, axis=1)
    ssd = bissd(ux, ubc, dt_raw, proj, dt_bias_row, a_log_row, d_skip_row, ssd_norm_w, geom)

    wo = w_out[0].astype(BF16)
    rw_hi, rw_lo = _split_hi_lo(pad_lanes(router_w[0]))
    x1, h2, logits = out_proj(na, ssd, xp, xs, wo[:NA_DIM], wo[NA_DIM:], norm_ffn_w,
                              rw_hi, rw_lo, pad_lanes(router_b))

    slot_tok, slot_w, block_e, n_used, pos = _route(logits[:, :N_EXPERTS])
    y = moe_experts(block_e, n_used, h2[slot_tok], w_gate[0], b_gate[0][:, None, :],
                    w_up[0], b_up[0][:, None, :], w_down[0], b_down[0][:, None, :], slot_w[:, None])
    out = combine_and_norm(x1, y[pos], final_norm_w[None, :])
    return (out[:b1 * s1].reshape(b1, s1, d), out[b1 * s1:].reshape(b2, s2, d))
```

```python
import functools
import math

import jax
import jax.numpy as jnp
from jax import lax
from jax.experimental import pallas as pl
from jax.experimental.pallas import tpu as pltpu

F32 = jnp.float32
BF16 = jnp.bfloat16

GRID_W = 64
NA_HEADS = 16
NA_HEAD_DIM = 64
NA_DIM = NA_HEADS * NA_HEAD_DIM
NA_ROWS = 8
NA_KW = 16
SSD_HEADS = 16
SSD_HEAD_DIM = 64
SSD_DIM = SSD_HEADS * SSD_HEAD_DIM
SSD_GROUPS = 2
SSD_STATE = 128
CONV_K = 5
BC_DIM = 2 * SSD_GROUPS * SSD_STATE
N_EXPERTS = 32
TOP_K = 4
SWIGLU_LIMIT = 7.0
SWIGLU_ALPHA = 1.702
RMS_EPS = 1e-5
PROJ_DIM = 3 * NA_DIM + SSD_DIM + SSD_DIM + BC_DIM
LANES = 128
NEG_BIG = -1e30

VMEM_LIMIT = 56 * 1024 * 1024


def _split_hi_lo(x):
    hi = x.astype(BF16)
    lo = (x - hi.astype(F32)).astype(BF16)
    return hi, lo


def _dot(a, b):
    return jnp.dot(a, b, preferred_element_type=F32)


IN_TM = 256
IN_CH = 512


def _in_proj_kernel(xp_ref, xs_ref, nw_ref, w_ref, wdt_hi_ref, wdt_lo_ref,
                    proj_ref, dt_ref, hhi_scr, hlo_scr, *, n_prompt_blocks):
    i = pl.program_id(0)

    def norm_from(x_ref):
        x = x_ref[...]
        ms = jnp.mean(x * x, axis=-1, keepdims=True)
        h = x * lax.rsqrt(ms + RMS_EPS) * nw_ref[...]
        hi, lo = _split_hi_lo(h)
        hhi_scr[...] = hi
        hlo_scr[...] = lo

    @pl.when(i < n_prompt_blocks)
    def _():
        norm_from(xp_ref)

    @pl.when(i >= n_prompt_blocks)
    def _():
        norm_from(xs_ref)

    h_hi = hhi_scr[...]
    for c in range(0, PROJ_DIM, IN_CH):
        proj_ref[:, c:c + IN_CH] = _dot(h_hi, w_ref[:, c:c + IN_CH]).astype(BF16)
    wdt_hi = wdt_hi_ref[...]
    dt_ref[...] = (_dot(h_hi, wdt_hi) + _dot(hlo_scr[...], wdt_hi) + _dot(h_hi, wdt_lo_ref[...]))


def _two_source_specs(tm, d, n_prompt_blocks):
    last = n_prompt_blocks - 1
    return (pl.BlockSpec((tm, d), lambda i: (jnp.minimum(i, last), 0)),
            pl.BlockSpec((tm, d), lambda i: (jnp.maximum(i - n_prompt_blocks, 0), 0)))


def _resident(shape):
    return pl.BlockSpec(shape, lambda i: (0,) * len(shape), pipeline_mode=pl.Buffered(1))


def in_proj(xp, xs, norm_w, w_main, wdt_hi, wdt_lo):
    tp, d = xp.shape
    ts = xs.shape[0]
    t = tp + ts
    npb = tp // IN_TM
    xp_spec, xs_spec = _two_source_specs(IN_TM, d, npb)
    return pl.pallas_call(
        functools.partial(_in_proj_kernel, n_prompt_blocks=npb),
        out_shape=(jax.ShapeDtypeStruct((t, PROJ_DIM), BF16),
                   jax.ShapeDtypeStruct((t, LANES), F32)),
        grid=(t // IN_TM,),
        in_specs=[xp_spec, xs_spec, _resident((1, d)), _resident((d, PROJ_DIM)),
                  _resident((d, LANES)), _resident((d, LANES))],
        out_specs=(pl.BlockSpec((IN_TM, PROJ_DIM), lambda i: (i, 0)),
                   pl.BlockSpec((IN_TM, LANES), lambda i: (i, 0))),
        scratch_shapes=[pltpu.VMEM((IN_TM, d), BF16), pltpu.VMEM((IN_TM, d), BF16)],
        compiler_params=pltpu.CompilerParams(dimension_semantics=("arbitrary",),
                                             vmem_limit_bytes=VMEM_LIMIT),
        name="in_proj",
    )(xp, xs, norm_w, w_main, wdt_hi, wdt_lo)


def _seq_start_len(pos, geom):
    b1, s1, b2, s2 = geom
    t1 = b1 * s1
    in_p = pos < t1
    start = jnp.where(in_p, (pos // s1) * s1, t1 + ((pos - t1) // s2) * s2)
    length = jnp.where(in_p, s1, s2)
    return start, length


NA_WIN = NA_ROWS * GRID_W
NA_PAIRS = NA_HEADS // 2


def _na_row_geometry(i, geom):
    start, length = _seq_start_len(i * GRID_W, geom)
    rows = length // GRID_W
    r = i - start // GRID_W
    r0 = jnp.clip(r - NA_ROWS // 2, 0, rows - NA_ROWS)
    return start + r0 * GRID_W, r - r0


def _na_kernel(q_ref, k_ref, v_ref, bias_ref, o_ref):
    lane = lax.broadcasted_iota(jnp.int32, (GRID_W, LANES), 1)
    first = lane < NA_HEAD_DIM
    zero = jnp.zeros((GRID_W, LANES), BF16)
    for p in range(NA_PAIRS):
        cols = slice(p * LANES, (p + 1) * LANES)
        qs = q_ref[:, cols] * jnp.asarray(NA_HEAD_DIM ** -0.5, BF16)
        qm = jnp.concatenate([jnp.where(first, qs, zero), jnp.where(first, zero, qs)], axis=0)
        s = lax.dot_general(qm, k_ref[:, cols], (((1,), (1,)), ((), ())),
                            preferred_element_type=F32)
        s = s + bias_ref[0, p]
        m = jnp.max(s, axis=-1, keepdims=True)
        e = jnp.exp(s - m)
        l = jnp.sum(e, axis=-1, keepdims=True)
        o = _dot(e.astype(BF16), v_ref[:, cols]) / l
        o_ref[:, cols] = jnp.where(first, o[:GRID_W], o[GRID_W:]).astype(BF16)


def na_bias_table(rpb):
    c = jnp.arange(GRID_W)
    w = jnp.arange(GRID_W)
    c0 = jnp.clip(c - NA_KW // 2, 0, GRID_W - NA_KW)
    valid = (w[None, :] >= c0[:, None]) & (w[None, :] < c0[:, None] + NA_KW)
    col_off = jnp.clip(w[None, :] - c[:, None] + NA_KW - 1, 0, 2 * NA_KW - 2)
    row_off = jnp.arange(NA_ROWS)[None, :] - jnp.arange(NA_ROWS)[:, None] + NA_ROWS - 1
    tab = rpb[:, row_off[:, :, None, None], col_off[None, None, :, :]]
    tab = jnp.where(valid[None, None, None], tab.astype(F32), NEG_BIG)
    tab = jnp.transpose(tab, (1, 0, 3, 2, 4))
    return tab.reshape(NA_ROWS, NA_PAIRS, 2 * GRID_W, NA_WIN)


def neighbourhood_attention(proj, bias_tab, geom):
    t = proj.shape[0]
    n_rows = t // GRID_W

    def win_map(col_block):
        return lambda i: (pl.multiple_of(_na_row_geometry(i, geom)[0], GRID_W), col_block * NA_DIM)

    win_block = (pl.Element(NA_WIN), pl.Element(NA_DIM))

    return pl.pallas_call(
        _na_kernel,
        out_shape=jax.ShapeDtypeStruct((t, NA_DIM), BF16),
        grid=(n_rows,),
        in_specs=[pl.BlockSpec((GRID_W, NA_DIM), lambda i: (i, 0)),
                  pl.BlockSpec(win_block, win_map(1)),
                  pl.BlockSpec(win_block, win_map(2)),
                  pl.BlockSpec((1, NA_PAIRS, 2 * GRID_W, NA_WIN),
                               lambda i: (_na_row_geometry(i, geom)[1], 0, 0, 0))],
        out_specs=pl.BlockSpec((GRID_W, NA_DIM), lambda i: (i, 0)),
        compiler_params=pltpu.CompilerParams(dimension_semantics=("arbitrary",),
                                             vmem_limit_bytes=VMEM_LIMIT),
        name="na_attention",
    )(proj, proj, proj, bias_tab)


CONV_TL = 256
HALO = 16
X_COL_BLOCK = (3 * NA_DIM + SSD_DIM) // SSD_DIM
BC_COL_BLOCK = (3 * NA_DIM + 2 * SSD_DIM) // BC_DIM
CONV_CH = 512


def _conv_kernel(xm_ref, xp_ref, xn_ref, bm_ref, bp_ref, bn_ref, w_ref, b_ref,
                 ux_ref, ubc_ref, ext_scr, *, geom):
    tl = CONV_TL
    pos = pl.program_id(0) * tl
    start, length = _seq_start_len(pos, geom)
    has_prev = pos > start
    has_next = pos + tl < start + length
    zx = jnp.zeros((HALO, SSD_DIM), F32)
    zb = jnp.zeros((HALO, BC_DIM), F32)
    ext_scr[0:HALO, 0:SSD_DIM] = jnp.where(has_prev, xp_ref[...].astype(F32), zx)
    ext_scr[0:HALO, SSD_DIM:] = jnp.where(has_prev, bp_ref[...].astype(F32), zb)
    ext_scr[HALO:HALO + tl, 0:SSD_DIM] = xm_ref[...].astype(F32)
    ext_scr[HALO:HALO + tl, SSD_DIM:] = bm_ref[...].astype(F32)
    ext_scr[HALO + tl:, 0:SSD_DIM] = jnp.where(has_next, xn_ref[...].astype(F32), zx)
    ext_scr[HALO + tl:, SSD_DIM:] = jnp.where(has_next, bn_ref[...].astype(F32), zb)
    for c0 in range(0, SSD_DIM + BC_DIM, CONV_CH):
        cols = slice(c0, c0 + CONV_CH)
        acc = jnp.broadcast_to(b_ref[:, cols], (tl, CONV_CH))
        for k in range(CONV_K):
            r0 = HALO - CONV_K // 2 + k
            acc = acc + w_ref[k:k + 1, cols] * ext_scr[r0:r0 + tl, cols]
        u = (acc * jax.nn.sigmoid(acc)).astype(BF16)
        if c0 < SSD_DIM:
            ux_ref[:, cols] = u
        else:
            ubc_ref[:, c0 - SSD_DIM:c0 - SSD_DIM + CONV_CH] = u


def conv_silu(proj, conv_w, conv_b, geom):
    t = proj.shape[0]
    tl = CONV_TL
    per = tl // HALO
    n_halo = t // HALO

    def prev_map(cb):
        return lambda i: (jnp.maximum(i * per - 1, 0), cb)

    def next_map(cb):
        return lambda i: (jnp.minimum((i + 1) * per, n_halo - 1), cb)

    c = SSD_DIM + BC_DIM
    return pl.pallas_call(
        functools.partial(_conv_kernel, geom=geom),
        out_shape=(jax.ShapeDtypeStruct((t, SSD_DIM), BF16), jax.ShapeDtypeStruct((t, BC_DIM), BF16)),
        grid=(t // tl,),
        in_specs=[pl.BlockSpec((tl, SSD_DIM), lambda i: (i, X_COL_BLOCK)),
                  pl.BlockSpec((HALO, SSD_DIM), prev_map(X_COL_BLOCK)),
                  pl.BlockSpec((HALO, SSD_DIM), next_map(X_COL_BLOCK)),
                  pl.BlockSpec((tl, BC_DIM), lambda i: (i, BC_COL_BLOCK)),
                  pl.BlockSpec((HALO, BC_DIM), prev_map(BC_COL_BLOCK)),
                  pl.BlockSpec((HALO, BC_DIM), next_map(BC_COL_BLOCK)),
                  pl.BlockSpec((8, c), lambda i: (0, 0)),
                  pl.BlockSpec((1, c), lambda i: (0, 0))],
        out_specs=(pl.BlockSpec((tl, SSD_DIM), lambda i: (i, 0)),
                   pl.BlockSpec((tl, BC_DIM), lambda i: (i, 0))),
        scratch_shapes=[pltpu.VMEM((tl + 2 * HALO, c), F32)],
        compiler_params=pltpu.CompilerParams(dimension_semantics=("arbitrary",),
                                             vmem_limit_bytes=VMEM_LIMIT),
        name="conv_silu",
    )(proj, proj, proj, proj, proj, proj, conv_w, conv_b)


SSD_L = 128
GROUP_DIM = SSD_DIM // SSD_GROUPS
HEADS_PER_GROUP = SSD_HEADS // SSD_GROUPS
BWD = SSD_HEADS


def _softplus(x):
    return jnp.maximum(x, 0.0) + jnp.log1p(jnp.exp(-jnp.abs(x)))


def _dt_and_a(dt_ref, dtbias_ref, alog_ref):
    dtv = _softplus(dt_ref[...] + dtbias_ref[...])
    return dtv, dtv * (-jnp.exp(alog_ref[...]))


def _tri_dot(tri, a):
    a_hi, a_lo = _split_hi_lo(a)
    return _dot(tri, a_hi) + _dot(tri, a_lo)


def _expand_heads(e, expand):
    e_hi, e_lo = _split_hi_lo(e)
    return _dot(e_hi, expand) + _dot(e_lo, expand)


def _cross_chunk(c_ref_cols, state_scr, g):
    sg = state_scr[g * GROUP_DIM:(g + 1) * GROUP_DIM, :].astype(BF16)
    return lax.dot_general(c_ref_cols, sg, (((1,), (1,)), ((), ())), preferred_element_type=F32)


def _update_state(state_scr, x_t, w_t, cd_t, b_cols, row_off):
    for g in range(SSD_GROUPS):
        parts = []
        for hh in range(HEADS_PER_GROUP):
            h = g * HEADS_PER_GROUP + hh
            parts.append((x_t[h * SSD_HEAD_DIM:(h + 1) * SSD_HEAD_DIM, :]
                          * w_t[row_off + h:row_off + h + 1, :]).astype(BF16))
        upd = _dot(jnp.concatenate(parts, axis=0), b_cols[g])
        for hh in range(HEADS_PER_GROUP):
            h = g * HEADS_PER_GROUP + hh
            rows = slice(h * SSD_HEAD_DIM, (h + 1) * SSD_HEAD_DIM)
            state_scr[rows, :] = (state_scr[rows, :] * cd_t[row_off + h:row_off + h + 1, 0:1]
                                  + upd[hh * SSD_HEAD_DIM:(hh + 1) * SSD_HEAD_DIM, :])


def _ssd_bwd_kernel(ux_ref, ubc_ref, dt_ref, dtbias_ref, alog_ref, expand_ref,
                    yoff_ref, state_scr, *, geom, n_chunks):
    pos = (n_chunks - 1 - pl.program_id(0)) * SSD_L
    start, length = _seq_start_len(pos, geom)

    @pl.when(pos + SSD_L == start + length)
    def _():
        state_scr[...] = jnp.zeros_like(state_scr)

    dtv, a = _dt_and_a(dt_ref, dtbias_ref, alog_ref)
    li = lax.broadcasted_iota(jnp.int32, (SSD_L, SSD_L), 0)
    si = lax.broadcasted_iota(jnp.int32, (SSD_L, SSD_L), 1)
    triu = jnp.where(si >= li, 1.0, 0.0).astype(BF16)
    rcum = _tri_dot(triu, a)
    e_full = _expand_heads(jnp.exp(rcum), expand_ref[...])
    n = SSD_STATE
    b_cols = [ubc_ref[:, g * n:(g + 1) * n] for g in range(SSD_GROUPS)]
    for g in range(SSD_GROUPS):
        c_g = ubc_ref[:, (SSD_GROUPS + g) * n:(SSD_GROUPS + g + 1) * n]
        cols = slice(g * GROUP_DIM, (g + 1) * GROUP_DIM)
        yoff_ref[:, cols] = _cross_chunk(c_g, state_scr, g) * e_full[:, cols]
    rcum_t = rcum.T
    w_t = jnp.exp(rcum_t[:, 0:1] - rcum_t) * dtv.T
    cd_t = jnp.exp(rcum_t[:, 0:1])
    _update_state(state_scr, ux_ref[...].astype(F32).T, w_t, cd_t, b_cols, BWD)


def _ssd_fwd_kernel(ux_ref, ubc_ref, dt_ref, z_ref, yoffb_ref, dtbias_ref, alog_ref, expand_ref,
                    dskip_ref, normw_ref, out_ref, state_scr, *, geom):
    pos = pl.program_id(0) * SSD_L
    start, _ = _seq_start_len(pos, geom)

    @pl.when(pos == start)
    def _():
        state_scr[...] = jnp.zeros_like(state_scr)

    dtv, a = _dt_and_a(dt_ref, dtbias_ref, alog_ref)
    li = lax.broadcasted_iota(jnp.int32, (SSD_L, SSD_L), 0)
    si = lax.broadcasted_iota(jnp.int32, (SSD_L, SSD_L), 1)
    tril = jnp.where(si <= li, 1.0, 0.0).astype(BF16)
    triu = jnp.where(si >= li, 1.0, 0.0).astype(BF16)
    cum = _tri_dot(tril, a)
    rcum = _tri_dot(triu, a)
    cum_t, rcum_t, dt_t = cum.T, rcum.T, dtv.T
    n = SSD_STATE
    b_cols = [ubc_ref[:, g * n:(g + 1) * n] for g in range(SSD_GROUPS)]
    c_cols = [ubc_ref[:, (SSD_GROUPS + g) * n:(SSD_GROUPS + g + 1) * n] for g in range(SSD_GROUPS)]
    gmat = [lax.dot_general(c_cols[g], b_cols[g], (((1,), (1,)), ((), ())),
                            preferred_element_type=F32) for g in range(SSD_GROUPS)]
    e_full = _expand_heads(jnp.exp(cum), expand_ref[...])
    cross = [_cross_chunk(c_cols[g], state_scr, g) for g in range(SSD_GROUPS)]
    lane = lax.broadcasted_iota(jnp.int32, (SSD_L, LANES), 1)
    lower = li > si
    upper = li < si
    gated = []
    for p in range(SSD_HEADS // 2):
        cols = slice(p * LANES, (p + 1) * LANES)
        mats = []
        for par in range(2):
            h = 2 * p + par
            g = h // HEADS_PER_GROUP
            seg = jnp.where(li >= si, cum[:, h:h + 1] - cum_t[h:h + 1, :],
                            rcum[:, BWD + h:BWD + h + 1] - rcum_t[BWD + h:BWD + h + 1, :])
            dt_f = dt_t[h:h + 1, :]
            dt_b = dt_t[BWD + h:BWD + h + 1, :]
            dsel = jnp.where(lower, dt_f, jnp.where(upper, dt_b, dt_f + dt_b))
            mats.append((gmat[g] * jnp.exp(seg) * dsel).astype(BF16))
        yp = _dot(jnp.concatenate(mats, axis=0), ux_ref[:, cols])
        y = jnp.where(lane < SSD_HEAD_DIM, yp[:SSD_L], yp[SSD_L:])
        g = (2 * p) // HEADS_PER_GROUP
        gc = slice(p * LANES - g * GROUP_DIM, (p + 1) * LANES - g * GROUP_DIM)
        y = y + cross[g][:, gc] * e_full[:, cols]
        y = y + yoffb_ref[:, cols] + dskip_ref[:, cols] * ux_ref[:, cols].astype(F32)
        zf = z_ref[:, cols].astype(F32)
        gated.append(y * (zf * jax.nn.sigmoid(zf)))
    per_group = len(gated) // SSD_GROUPS
    for g in range(SSD_GROUPS):
        mine = gated[g * per_group:(g + 1) * per_group]
        ssq = sum(jnp.sum(v * v, axis=-1, keepdims=True) for v in mine)
        scale = lax.rsqrt(ssq / GROUP_DIM + RMS_EPS)
        for j, v in enumerate(mine):
            cols = slice((g * per_group + j) * LANES, (g * per_group + j + 1) * LANES)
            out_ref[:, cols] = (v * scale * normw_ref[:, cols]).astype(BF16)
    w_t = jnp.exp(cum_t[:, SSD_L - 1:SSD_L] - cum_t) * dt_t
    cd_t = jnp.exp(cum_t[:, SSD_L - 1:SSD_L])
    _update_state(state_scr, ux_ref[...].astype(F32).T, w_t, cd_t, b_cols, 0)


def _head_expand_matrix(row_off):
    h = jnp.arange(LANES)[:, None]
    c = jnp.arange(SSD_DIM)[None, :] // SSD_HEAD_DIM
    return jnp.where(h == c + row_off, 1.0, 0.0).astype(BF16)


def bissd(ux, ubc, dt_raw, proj, dt_bias_row, a_log_row, d_skip_row, norm_w_row, geom):
    t = ux.shape[0]
    n_chunks = t // SSD_L
    cp = pltpu.CompilerParams(dimension_semantics=("arbitrary",), vmem_limit_bytes=VMEM_LIMIT)
    row128 = pl.BlockSpec((1, LANES), lambda i: (0, 0))
    row_dim = pl.BlockSpec((1, SSD_DIM), lambda i: (0, 0))
    expand_spec = pl.BlockSpec((LANES, SSD_DIM), lambda i: (0, 0))
    state = pltpu.VMEM((SSD_DIM, SSD_STATE), F32)

    def rev(width):
        return pl.BlockSpec((SSD_L, width), lambda i: (n_chunks - 1 - i, 0))

    def fwd(width, col_block=0):
        return pl.BlockSpec((SSD_L, width), lambda i: (i, col_block))

    yoffb = pl.pallas_call(
        functools.partial(_ssd_bwd_kernel, geom=geom, n_chunks=n_chunks),
        out_shape=jax.ShapeDtypeStruct((t, SSD_DIM), F32),
        grid=(n_chunks,),
        in_specs=[rev(SSD_DIM), rev(BC_DIM), rev(LANES), row128, row128, expand_spec],
        out_specs=rev(SSD_DIM),
        scratch_shapes=[state],
        compiler_params=cp,
        name="ssd_backward",
    )(ux, ubc, dt_raw, dt_bias_row, a_log_row, _head_expand_matrix(BWD))
    z_col_block = 3 * NA_DIM // SSD_DIM
    return pl.pallas_call(
        functools.partial(_ssd_fwd_kernel, geom=geom),
        out_shape=jax.ShapeDtypeStruct((t, SSD_DIM), BF16),
        grid=(n_chunks,),
        in_specs=[fwd(SSD_DIM), fwd(BC_DIM), fwd(LANES), fwd(SSD_DIM, z_col_block), fwd(SSD_DIM),
                  row128, row128, expand_spec, row_dim, row_dim],
        out_specs=fwd(SSD_DIM),
        scratch_shapes=[state],
        compiler_params=cp,
        name="ssd_forward",
    )(ux, ubc, dt_raw, proj, yoffb, dt_bias_row, a_log_row, _head_expand_matrix(0),
      d_skip_row, norm_w_row)


OUT_TM = 256


def _out_proj_kernel(na_ref, ssd_ref, xp_ref, xs_ref, wna_ref, wssd_ref, nw_ref,
                     rw_hi_ref, rw_lo_ref, rb_ref, x1_ref, h2_ref, logit_ref, *, n_prompt_blocks):
    i = pl.program_id(0)
    mix = _dot(na_ref[...], wna_ref[...]) + _dot(ssd_ref[...], wssd_ref[...])

    @pl.when(i < n_prompt_blocks)
    def _():
        x1_ref[...] = xp_ref[...] + mix

    @pl.when(i >= n_prompt_blocks)
    def _():
        x1_ref[...] = xs_ref[...] + mix

    x1 = x1_ref[...]
    ms = jnp.mean(x1 * x1, axis=-1, keepdims=True)
    h2 = x1 * lax.rsqrt(ms + RMS_EPS) * nw_ref[...]
    hi, lo = _split_hi_lo(h2)
    h2_ref[...] = hi
    rw_hi = rw_hi_ref[...]
    logit_ref[...] = _dot(hi, rw_hi) + _dot(lo, rw_hi) + _dot(hi, rw_lo_ref[...]) + rb_ref[...]


def out_proj(na, ssd, xp, xs, w_na, w_ssd, norm_w, rw_hi, rw_lo, rb):
    t = na.shape[0]
    d = xp.shape[1]
    tm = OUT_TM
    npb = xp.shape[0] // tm
    xp_spec, xs_spec = _two_source_specs(tm, d, npb)
    row = lambda w: pl.BlockSpec((tm, w), lambda i: (i, 0))
    return pl.pallas_call(
        functools.partial(_out_proj_kernel, n_prompt_blocks=npb),
        out_shape=(jax.ShapeDtypeStruct((t, d), F32), jax.ShapeDtypeStruct((t, d), BF16),
                   jax.ShapeDtypeStruct((t, LANES), F32)),
        grid=(t // tm,),
        in_specs=[row(NA_DIM), row(SSD_DIM), xp_spec, xs_spec,
                  _resident((NA_DIM, d)), _resident((SSD_DIM, d)), _resident((1, d)),
                  _resident((d, LANES)), _resident((d, LANES)), _resident((1, LANES))],
        out_specs=(row(d), row(d), row(LANES)),
        compiler_params=pltpu.CompilerParams(dimension_semantics=("arbitrary",),
                                             vmem_limit_bytes=VMEM_LIMIT),
        name="out_proj",
    )(na, ssd, xp, xs, w_na, w_ssd, norm_w, rw_hi, rw_lo, rb)


MOE_BLK = 512
MOE_TF = 512


def _moe_kernel(be_ref, nused_ref, x_ref, wg_ref, bg_ref, wu_ref, bu_ref, wd_ref, bd_ref, sw_ref,
                y_ref, acc_scr, *, n_f):
    b = pl.program_id(0)
    f = pl.program_id(1)
    used = b < nused_ref[0]

    @pl.when(used)
    def _():
        x = x_ref[...]
        g = jnp.minimum(_dot(x, wg_ref[0].astype(BF16)) + bg_ref[0], SWIGLU_LIMIT)
        u = jnp.clip(_dot(x, wu_ref[0].astype(BF16)) + bu_ref[0], -SWIGLU_LIMIT, SWIGLU_LIMIT)
        act = (u + 1.0) * (g * jax.nn.sigmoid(SWIGLU_ALPHA * g))
        part = _dot(act.astype(BF16), wd_ref[0].astype(BF16))

        @pl.when(f == 0)
        def _():
            acc_scr[...] = part

        @pl.when(f > 0)
        def _():
            acc_scr[...] += part

        @pl.when(f == n_f - 1)
        def _():
            y_ref[...] = ((acc_scr[...] + bd_ref[0]) * sw_ref[...]).astype(BF16)

    @pl.when(jnp.logical_and(jnp.logical_not(used), f == n_f - 1))
    def _():
        y_ref[...] = jnp.zeros_like(y_ref)


def moe_experts(block_e, n_used, xs, w_gate, b_gate, w_up, b_up, w_down, b_down, slot_w):
    slots, d = xs.shape
    d_ff = w_gate.shape[2]
    n_blocks = slots // MOE_BLK
    n_f = d_ff // MOE_TF

    def f_idx(b, f, nu):
        return jnp.where(b < nu[0], f, 0)

    grid_spec = pltpu.PrefetchScalarGridSpec(
        num_scalar_prefetch=2,
        grid=(n_blocks, n_f),
        in_specs=[pl.BlockSpec((MOE_BLK, d), lambda b, f, be, nu: (b, 0)),
                  pl.BlockSpec((1, d, MOE_TF), lambda b, f, be, nu: (be[b], 0, f_idx(b, f, nu))),
                  pl.BlockSpec((1, 1, MOE_TF), lambda b, f, be, nu: (be[b], 0, f_idx(b, f, nu))),
                  pl.BlockSpec((1, d, MOE_TF), lambda b, f, be, nu: (be[b], 0, f_idx(b, f, nu))),
                  pl.BlockSpec((1, 1, MOE_TF), lambda b, f, be, nu: (be[b], 0, f_idx(b, f, nu))),
                  pl.BlockSpec((1, MOE_TF, d), lambda b, f, be, nu: (be[b], f_idx(b, f, nu), 0)),
                  pl.BlockSpec((1, 1, d), lambda b, f, be, nu: (be[b], 0, 0)),
                  pl.BlockSpec((MOE_BLK, 1), lambda b, f, be, nu: (b, 0))],
        out_specs=pl.BlockSpec((MOE_BLK, d), lambda b, f, be, nu: (b, 0)),
        scratch_shapes=[pltpu.VMEM((MOE_BLK, d), F32)],
    )
    return pl.pallas_call(
        functools.partial(_moe_kernel, n_f=n_f),
        out_shape=jax.ShapeDtypeStruct((slots, d), BF16),
        grid_spec=grid_spec,
        compiler_params=pltpu.CompilerParams(dimension_semantics=("arbitrary", "arbitrary"),
                                             vmem_limit_bytes=VMEM_LIMIT),
        name="moe_experts",
    )(block_e, n_used, xs, w_gate, b_gate, w_up, b_up, w_down, b_down, slot_w)


FIN_TM = 256


def _final_kernel(x1_ref, *refs):
    y_refs, nw_ref, o_ref = refs[:TOP_K], refs[TOP_K], refs[TOP_K + 1]
    x = x1_ref[...]
    for y_ref in y_refs:
        x = x + y_ref[...].astype(F32)
    ms = jnp.mean(x * x, axis=-1, keepdims=True)
    o_ref[...] = x * lax.rsqrt(ms + RMS_EPS) * nw_ref[...]


def combine_and_norm(x1, yg, norm_w):
    t, d = x1.shape
    tm = FIN_TM
    per_k = t // tm

    def choice(k):
        return pl.BlockSpec((tm, d), lambda i: (k * per_k + i, 0))

    return pl.pallas_call(
        _final_kernel,
        out_shape=jax.ShapeDtypeStruct((t, d), F32),
        grid=(per_k,),
        in_specs=[pl.BlockSpec((tm, d), lambda i: (i, 0))] + [choice(k) for k in range(TOP_K)]
                 + [pl.BlockSpec((1, d), lambda i: (0, 0))],
        out_specs=pl.BlockSpec((tm, d), lambda i: (i, 0)),
        compiler_params=pltpu.CompilerParams(dimension_semantics=("arbitrary",),
                                             vmem_limit_bytes=VMEM_LIMIT),
        name="combine_norm",
    )(x1, *([yg] * TOP_K), norm_w)


def _route(logits):
    t = logits.shape[0]
    m = t * TOP_K
    top_logit, top_e = lax.top_k(logits, TOP_K)
    gate_w = jax.nn.softmax(top_logit, axis=-1)
    flat_e = top_e.reshape(m).astype(jnp.int32)
    iota = jnp.arange(m, dtype=jnp.int32)
    e_sorted, order, w_sorted = lax.sort((flat_e, iota, gate_w.reshape(m)), num_keys=1, is_stable=True)
    counts = jnp.bincount(flat_e, length=N_EXPERTS).astype(jnp.int32)
    padded = (counts + MOE_BLK - 1) // MOE_BLK * MOE_BLK
    pad_end = jnp.cumsum(padded)
    pad_start = pad_end - padded
    grp_start = jnp.cumsum(counts) - counts
    dest = pad_start[e_sorted] + iota - grp_start[e_sorted]
    n_blocks = -(-m // MOE_BLK) + N_EXPERTS
    block_e = jnp.minimum(jnp.searchsorted(pad_end, jnp.arange(n_blocks, dtype=jnp.int32) * MOE_BLK,
                                           side='right'), N_EXPERTS - 1).astype(jnp.int32)
    n_used = (pad_end[-1:] // MOE_BLK).astype(jnp.int32)
    e_slot = jnp.repeat(block_e, MOE_BLK)
    off = jnp.arange(n_blocks * MOE_BLK, dtype=jnp.int32) - pad_start[e_slot]
    valid = off < counts[e_slot]
    src = jnp.clip(grp_start[e_slot] + off, 0, m - 1)
    slot_tok = jnp.where(valid, order[src] // TOP_K, 0)
    slot_w = jnp.where(valid, w_sorted[src], 0.0)
    _, pos = lax.sort((order, dest), num_keys=1)
    pos = pos.reshape(t, TOP_K).T.reshape(m)
    return slot_tok, slot_w, block_e, n_used, pos


def kernel(x_prompt, x_sample, norm_mix_w, w_in, na_rpb, conv_w, conv_b, dt_bias_fwd, dt_bias_bwd,
           a_log_fwd, a_log_bwd, d_skip, ssd_norm_w, w_out, norm_ffn_w, router_w, router_b,
           w_gate, b_gate, w_up, b_up, w_down, b_down, final_norm_w):
    b1, s1, d = x_prompt.shape
    b2, s2, _ = x_sample.shape
    geom = (b1, s1, b2, s2)
    for s in (s1, s2):
        assert s % (NA_ROWS * GRID_W) == 0 and s % CONV_TL == 0 and s % SSD_L == 0
    assert w_in.shape[0] == 1, "one layer"
    xp = x_prompt.reshape(b1 * s1, d)
    xs = x_sample.reshape(b2 * s2, d)
    t = xp.shape[0] + xs.shape[0]

    def pad_lanes(a):
        return jnp.pad(a, ((0, 0), (0, LANES - a.shape[1])))

    wi = w_in[0]
    wdt_hi, wdt_lo = _split_hi_lo(pad_lanes(wi[:, PROJ_DIM:]))
    proj, dt_raw = in_proj(xp, xs, norm_mix_w, wi[:, :PROJ_DIM].astype(BF16), wdt_hi, wdt_lo)

    na = neighbourhood_attention(proj, na_bias_table(na_rpb[0]), geom)

    ux, ubc = conv_silu(proj, jnp.pad(conv_w[0], ((0, 8 - CONV_K), (0, 0))), conv_b, geom)
    dt_bias_row = pad_lanes(jnp.concatenate([dt_bias_fwd, dt_bias_bwd], axis=1))
    a_log_row = pad_lanes(jnp.concatenate([a_log_fwd, a_log_bwd], axis=1))
    d_skip_row = jnp.repeat(d_skip, SSD_HEAD_DIM, axis=1)
    ssd = bissd(ux, ubc, dt_raw, proj, dt_bias_row, a_log_row, d_skip_row, ssd_norm_w, geom)

    wo = w_out[0].astype(BF16)
    rw_hi, rw_lo = _split_hi_lo(pad_lanes(router_w[0]))
    x1, h2, logits = out_proj(na, ssd, xp, xs, wo[:NA_DIM], wo[NA_DIM:], norm_ffn_w,
                              rw_hi, rw_lo, pad_lanes(router_b))

    slot_tok, slot_w, block_e, n_used, pos = _route(logits[:, :N_EXPERTS])
    y = moe_experts(block_e, n_used, h2[slot_tok], w_gate[0], b_gate[0][:, None, :],
                    w_up[0], b_up[0][:, None, :], w_down[0], b_down[0][:, None, :], slot_w[:, None])
    out = combine_and_norm(x1, y[pos], final_norm_w[None, :])
    return (out[:b1 * s1].reshape(b1, s1, d), out[b1 * s1:].reshape(b2, s2, d))
```

```python
import functools
import math

import jax
import jax.numpy as jnp
from jax import lax
from jax.experimental import pallas as pl
from jax.experimental.pallas import tpu as pltpu

F32 = jnp.float32
BF16 = jnp.bfloat16

GRID_W = 64
NA_HEADS = 16
NA_HEAD_DIM = 64
NA_DIM = NA_HEADS * NA_HEAD_DIM
NA_ROWS = 8
NA_KW = 16
SSD_HEADS = 16
SSD_HEAD_DIM = 64
SSD_DIM = SSD_HEADS * SSD_HEAD_DIM
SSD_GROUPS = 2
SSD_STATE = 128
CONV_K = 5
BC_DIM = 2 * SSD_GROUPS * SSD_STATE
N_EXPERTS = 32
TOP_K = 4
SWIGLU_LIMIT = 7.0
SWIGLU_ALPHA = 1.702
RMS_EPS = 1e-5
PROJ_DIM = 3 * NA_DIM + SSD_DIM + SSD_DIM + BC_DIM
LANES = 128
NEG_BIG = -1e30

VMEM_LIMIT = 56 * 1024 * 1024


def _split_hi_lo(x):
    hi = x.astype(BF16)
    lo = (x - hi.astype(F32)).astype(BF16)
    return hi, lo


def _dot(a, b):
    return jnp.dot(a, b, preferred_element_type=F32)


IN_TM = 256
IN_CH = 512


def _in_proj_kernel(xp_ref, xs_ref, nw_ref, w_ref, wdt_hi_ref, wdt_lo_ref,
                    proj_ref, dt_ref, hhi_scr, hlo_scr, *, n_prompt_blocks):
    i = pl.program_id(0)

    def norm_from(x_ref):
        x = x_ref[...]
        ms = jnp.mean(x * x, axis=-1, keepdims=True)
        h = x * lax.rsqrt(ms + RMS_EPS) * nw_ref[...]
        hi, lo = _split_hi_lo(h)
        hhi_scr[...] = hi
        hlo_scr[...] = lo

    @pl.when(i < n_prompt_blocks)
    def _():
        norm_from(xp_ref)

    @pl.when(i >= n_prompt_blocks)
    def _():
        norm_from(xs_ref)

    h_hi = hhi_scr[...]
    for c in range(0, PROJ_DIM, IN_CH):
        proj_ref[:, c:c + IN_CH] = _dot(h_hi, w_ref[:, c:c + IN_CH]).astype(BF16)
    wdt_hi = wdt_hi_ref[...]
    dt_ref[...] = (_dot(h_hi, wdt_hi) + _dot(hlo_scr[...], wdt_hi) + _dot(h_hi, wdt_lo_ref[...]))


def _two_source_specs(tm, d, n_prompt_blocks):
    last = n_prompt_blocks - 1
    return (pl.BlockSpec((tm, d), lambda i: (jnp.minimum(i, last), 0)),
            pl.BlockSpec((tm, d), lambda i: (jnp.maximum(i - n_prompt_blocks, 0), 0)))


def _resident(shape):
    return pl.BlockSpec(shape, lambda i: (0,) * len(shape), pipeline_mode=pl.Buffered(1))


def in_proj(xp, xs, norm_w, w_main, wdt_hi, wdt_lo):
    tp, d = xp.shape
    ts = xs.shape[0]
    t = tp + ts
    npb = tp // IN_TM
    xp_spec, xs_spec = _two_source_specs(IN_TM, d, npb)
    return pl.pallas_call(
        functools.partial(_in_proj_kernel, n_prompt_blocks=npb),
        out_shape=(jax.ShapeDtypeStruct((t, PROJ_DIM), BF16),
                   jax.ShapeDtypeStruct((t, LANES), F32)),
        grid=(t // IN_TM,),
        in_specs=[xp_spec, xs_spec, _resident((1, d)), _resident((d, PROJ_DIM)),
                  _resident((d, LANES)), _resident((d, LANES))],
        out_specs=(pl.BlockSpec((IN_TM, PROJ_DIM), lambda i: (i, 0)),
                   pl.BlockSpec((IN_TM, LANES), lambda i: (i, 0))),
        scratch_shapes=[pltpu.VMEM((IN_TM, d), BF16), pltpu.VMEM((IN_TM, d), BF16)],
        compiler_params=pltpu.CompilerParams(dimension_semantics=("arbitrary",),
                                             vmem_limit_bytes=VMEM_LIMIT),
        name="in_proj",
    )(xp, xs, norm_w, w_main, wdt_hi, wdt_lo)


def _seq_start_len(pos, geom):
    b1, s1, b2, s2 = geom
    t1 = b1 * s1
    in_p = pos < t1
    start = jnp.where(in_p, (pos // s1) * s1, t1 + ((pos - t1) // s2) * s2)
    length = jnp.where(in_p, s1, s2)
    return start, length


NA_WIN = NA_ROWS * GRID_W
NA_PAIRS = NA_HEADS // 2


def _na_row_geometry(i, geom):
    start, length = _seq_start_len(i * GRID_W, geom)
    rows = length // GRID_W
    r = i - start // GRID_W
    r0 = jnp.clip(r - NA_ROWS // 2, 0, rows - NA_ROWS)
    return start + r0 * GRID_W, r - r0


def _na_kernel(q_ref, k_ref, v_ref, bias_ref, o_ref):
    lane = lax.broadcasted_iota(jnp.int32, (GRID_W, LANES), 1)
    first = lane < NA_HEAD_DIM
    zero = jnp.zeros((GRID_W, LANES), BF16)
    for p in range(NA_PAIRS):
        cols = slice(p * LANES, (p + 1) * LANES)
        qs = q_ref[:, cols] * jnp.asarray(NA_HEAD_DIM ** -0.5, BF16)
        qm = jnp.concatenate([jnp.where(first, qs, zero), jnp.where(first, zero, qs)], axis=0)
        s = lax.dot_general(qm, k_ref[:, cols], (((1,), (1,)), ((), ())),
                            preferred_element_type=F32)
        s = s + bias_ref[0, p]
        m = jnp.max(s, axis=-1, keepdims=True)
        e = jnp.exp(s - m)
        l = jnp.sum(e, axis=-1, keepdims=True)
        o = _dot(e.astype(BF16), v_ref[:, cols]) / l
        o_ref[:, cols] = jnp.where(first, o[:GRID_W], o[GRID_W:]).astype(BF16)


def na_bias_table(rpb):
    c = jnp.arange(GRID_W)
    w = jnp.arange(GRID_W)
    c0 = jnp.clip(c - NA_KW // 2, 0, GRID_W - NA_KW)
    valid = (w[None, :] >= c0[:, None]) & (w[None, :] < c0[:, None] + NA_KW)
    pad = GRID_W - NA_KW
    rp = jnp.pad(rpb.astype(F32), ((0, 0), (0, 0), (pad, pad)))
    tab = jnp.stack([rp[:, :, GRID_W - 1 - ci:2 * GRID_W - 1 - ci] for ci in range(GRID_W)], axis=1)
    tab = jnp.where(valid[None, :, None, :], tab, NEG_BIG)
    tab = jnp.stack([tab[:, :, NA_ROWS - 1 - dl:2 * NA_ROWS - 1 - dl, :] for dl in range(NA_ROWS)], axis=0)
    return tab.reshape(NA_ROWS, NA_PAIRS, 2 * GRID_W, NA_WIN)


def neighbourhood_attention(proj, bias_tab, geom):
    t = proj.shape[0]
    n_rows = t // GRID_W

    def win_map(col_block):
        return lambda i: (pl.multiple_of(_na_row_geometry(i, geom)[0], GRID_W), col_block * NA_DIM)

    win_block = (pl.Element(NA_WIN), pl.Element(NA_DIM))

    return pl.pallas_call(
        _na_kernel,
        out_shape=jax.ShapeDtypeStruct((t, NA_DIM), BF16),
        grid=(n_rows,),
        in_specs=[pl.BlockSpec((GRID_W, NA_DIM), lambda i: (i, 0)),
                  pl.BlockSpec(win_block, win_map(1)),
                  pl.BlockSpec(win_block, win_map(2)),
                  pl.BlockSpec((1, NA_PAIRS, 2 * GRID_W, NA_WIN),
                               lambda i: (_na_row_geometry(i, geom)[1], 0, 0, 0))],
        out_specs=pl.BlockSpec((GRID_W, NA_DIM), lambda i: (i, 0)),
        compiler_params=pltpu.CompilerParams(dimension_semantics=("arbitrary",),
                                             vmem_limit_bytes=VMEM_LIMIT),
        name="na_attention",
    )(proj, proj, proj, bias_tab)


CONV_TL = 256
HALO = 16
X_COL_BLOCK = (3 * NA_DIM + SSD_DIM) // SSD_DIM
BC_COL_BLOCK = (3 * NA_DIM + 2 * SSD_DIM) // BC_DIM
CONV_CH = 512


def _conv_kernel(xm_ref, xp_ref, xn_ref, bm_ref, bp_ref, bn_ref, w_ref, b_ref,
                 ux_ref, ubc_ref, ext_scr, *, geom):
    tl = CONV_TL
    pos = pl.program_id(0) * tl
    start, length = _seq_start_len(pos, geom)
    has_prev = pos > start
    has_next = pos + tl < start + length
    zx = jnp.zeros((HALO, SSD_DIM), F32)
    zb = jnp.zeros((HALO, BC_DIM), F32)
    ext_scr[0:HALO, 0:SSD_DIM] = jnp.where(has_prev, xp_ref[...].astype(F32), zx)
    ext_scr[0:HALO, SSD_DIM:] = jnp.where(has_prev, bp_ref[...].astype(F32), zb)
    ext_scr[HALO:HALO + tl, 0:SSD_DIM] = xm_ref[...].astype(F32)
    ext_scr[HALO:HALO + tl, SSD_DIM:] = bm_ref[...].astype(F32)
    ext_scr[HALO + tl:, 0:SSD_DIM] = jnp.where(has_next, xn_ref[...].astype(F32), zx)
    ext_scr[HALO + tl:, SSD_DIM:] = jnp.where(has_next, bn_ref[...].astype(F32), zb)
    for c0 in range(0, SSD_DIM + BC_DIM, CONV_CH):
        cols = slice(c0, c0 + CONV_CH)
        acc = jnp.broadcast_to(b_ref[:, cols], (tl, CONV_CH))
        for k in range(CONV_K):
            r0 = HALO - CONV_K // 2 + k
            acc = acc + w_ref[k:k + 1, cols] * ext_scr[r0:r0 + tl, cols]
        u = (acc * jax.nn.sigmoid(acc)).astype(BF16)
        if c0 < SSD_DIM:
            ux_ref[:, cols] = u
        else:
            ubc_ref[:, c0 - SSD_DIM:c0 - SSD_DIM + CONV_CH] = u


def conv_silu(proj, conv_w, conv_b, geom):
    t = proj.shape[0]
    tl = CONV_TL
    per = tl // HALO
    n_halo = t // HALO

    def prev_map(cb):
        return lambda i: (jnp.maximum(i * per - 1, 0), cb)

    def next_map(cb):
        return lambda i: (jnp.minimum((i + 1) * per, n_halo - 1), cb)

    c = SSD_DIM + BC_DIM
    return pl.pallas_call(
        functools.partial(_conv_kernel, geom=geom),
        out_shape=(jax.ShapeDtypeStruct((t, SSD_DIM), BF16), jax.ShapeDtypeStruct((t, BC_DIM), BF16)),
        grid=(t // tl,),
        in_specs=[pl.BlockSpec((tl, SSD_DIM), lambda i: (i, X_COL_BLOCK)),
                  pl.BlockSpec((HALO, SSD_DIM), prev_map(X_COL_BLOCK)),
                  pl.BlockSpec((HALO, SSD_DIM), next_map(X_COL_BLOCK)),
                  pl.BlockSpec((tl, BC_DIM), lambda i: (i, BC_COL_BLOCK)),
                  pl.BlockSpec((HALO, BC_DIM), prev_map(BC_COL_BLOCK)),
                  pl.BlockSpec((HALO, BC_DIM), next_map(BC_COL_BLOCK)),
                  pl.BlockSpec((8, c), lambda i: (0, 0)),
                  pl.BlockSpec((1, c), lambda i: (0, 0))],
        out_specs=(pl.BlockSpec((tl, SSD_DIM), lambda i: (i, 0)),
                   pl.BlockSpec((tl, BC_DIM), lambda i: (i, 0))),
        scratch_shapes=[pltpu.VMEM((tl + 2 * HALO, c), F32)],
        compiler_params=pltpu.CompilerParams(dimension_semantics=("arbitrary",),
                                             vmem_limit_bytes=VMEM_LIMIT),
        name="conv_silu",
    )(proj, proj, proj, proj, proj, proj, conv_w, conv_b)


SSD_L = 128
GROUP_DIM = SSD_DIM // SSD_GROUPS
HEADS_PER_GROUP = SSD_HEADS // SSD_GROUPS
BWD = SSD_HEADS


def _softplus(x):
    return jnp.maximum(x, 0.0) + jnp.log1p(jnp.exp(-jnp.abs(x)))


def _dt_and_a(dt_ref, dtbias_ref, alog_ref):
    dtv = _softplus(dt_ref[...] + dtbias_ref[...])
    return dtv, dtv * (-jnp.exp(alog_ref[...]))


def _tri_dot(tri, a):
    a_hi, a_lo = _split_hi_lo(a)
    return _dot(tri, a_hi) + _dot(tri, a_lo)


def _expand_heads(e, expand):
    e_hi, e_lo = _split_hi_lo(e)
    return _dot(e_hi, expand) + _dot(e_lo, expand)


def _cross_chunk(c_ref_cols, state_scr, g):
    sg = state_scr[g * GROUP_DIM:(g + 1) * GROUP_DIM, :].astype(BF16)
    return lax.dot_general(c_ref_cols, sg, (((1,), (1,)), ((), ())), preferred_element_type=F32)


def _update_state(state_scr, x_t, w_t, cd_t, b_cols, row_off):
    for g in range(SSD_GROUPS):
        parts = []
        for hh in range(HEADS_PER_GROUP):
            h = g * HEADS_PER_GROUP + hh
            parts.append((x_t[h * SSD_HEAD_DIM:(h + 1) * SSD_HEAD_DIM, :]
                          * w_t[row_off + h:row_off + h + 1, :]).astype(BF16))
        upd = _dot(jnp.concatenate(parts, axis=0), b_cols[g])
        for hh in range(HEADS_PER_GROUP):
            h = g * HEADS_PER_GROUP + hh
            rows = slice(h * SSD_HEAD_DIM, (h + 1) * SSD_HEAD_DIM)
            state_scr[rows, :] = (state_scr[rows, :] * cd_t[row_off + h:row_off + h + 1, 0:1]
                                  + upd[hh * SSD_HEAD_DIM:(hh + 1) * SSD_HEAD_DIM, :])


def _ssd_bwd_kernel(ux_ref, ubc_ref, dt_ref, dtbias_ref, alog_ref, expand_ref,
                    yoff_ref, state_scr, *, geom, n_chunks):
    pos = (n_chunks - 1 - pl.program_id(0)) * SSD_L
    start, length = _seq_start_len(pos, geom)

    @pl.when(pos + SSD_L == start + length)
    def _():
        state_scr[...] = jnp.zeros_like(state_scr)

    dtv, a = _dt_and_a(dt_ref, dtbias_ref, alog_ref)
    li = lax.broadcasted_iota(jnp.int32, (SSD_L, SSD_L), 0)
    si = lax.broadcasted_iota(jnp.int32, (SSD_L, SSD_L), 1)
    triu = jnp.where(si >= li, 1.0, 0.0).astype(BF16)
    rcum = _tri_dot(triu, a)
    e_full = _expand_heads(jnp.exp(rcum), expand_ref[...])
    n = SSD_STATE
    b_cols = [ubc_ref[:, g * n:(g + 1) * n] for g in range(SSD_GROUPS)]
    for g in range(SSD_GROUPS):
        c_g = ubc_ref[:, (SSD_GROUPS + g) * n:(SSD_GROUPS + g + 1) * n]
        cols = slice(g * GROUP_DIM, (g + 1) * GROUP_DIM)
        yoff_ref[:, cols] = _cross_chunk(c_g, state_scr, g) * e_full[:, cols]
    rcum_t = rcum.T
    w_t = jnp.exp(rcum_t[:, 0:1] - rcum_t) * dtv.T
    cd_t = jnp.exp(rcum_t[:, 0:1])
    _update_state(state_scr, ux_ref[...].astype(F32).T, w_t, cd_t, b_cols, BWD)


def _ssd_fwd_kernel(ux_ref, ubc_ref, dt_ref, z_ref, yoffb_ref, dtbias_ref, alog_ref, expand_ref,
                    dskip_ref, normw_ref, out_ref, state_scr, *, geom):
    pos = pl.program_id(0) * SSD_L
    start, _ = _seq_start_len(pos, geom)

    @pl.when(pos == start)
    def _():
        state_scr[...] = jnp.zeros_like(state_scr)

    dtv, a = _dt_and_a(dt_ref, dtbias_ref, alog_ref)
    li = lax.broadcasted_iota(jnp.int32, (SSD_L, SSD_L), 0)
    si = lax.broadcasted_iota(jnp.int32, (SSD_L, SSD_L), 1)
    tril = jnp.where(si <= li, 1.0, 0.0).astype(BF16)
    triu = jnp.where(si >= li, 1.0, 0.0).astype(BF16)
    cum = _tri_dot(tril, a)
    rcum = _tri_dot(triu, a)
    cum_t, rcum_t, dt_t = cum.T, rcum.T, dtv.T
    n = SSD_STATE
    b_cols = [ubc_ref[:, g * n:(g + 1) * n] for g in range(SSD_GROUPS)]
    c_cols = [ubc_ref[:, (SSD_GROUPS + g) * n:(SSD_GROUPS + g + 1) * n] for g in range(SSD_GROUPS)]
    gmat = [lax.dot_general(c_cols[g], b_cols[g], (((1,), (1,)), ((), ())),
                            preferred_element_type=F32) for g in range(SSD_GROUPS)]
    e_full = _expand_heads(jnp.exp(cum), expand_ref[...])
    cross = [_cross_chunk(c_cols[g], state_scr, g) for g in range(SSD_GROUPS)]
    lane = lax.broadcasted_iota(jnp.int32, (SSD_L, LANES), 1)
    lower = li > si
    upper = li < si
    gated = []
    for p in range(SSD_HEADS // 2):
        cols = slice(p * LANES, (p + 1) * LANES)
        mats = []
        for par in range(2):
            h = 2 * p + par
            g = h // HEADS_PER_GROUP
            seg = jnp.where(li >= si, cum[:, h:h + 1] - cum_t[h:h + 1, :],
                            rcum[:, BWD + h:BWD + h + 1] - rcum_t[BWD + h:BWD + h + 1, :])
            dt_f = dt_t[h:h + 1, :]
            dt_b = dt_t[BWD + h:BWD + h + 1, :]
            dsel = jnp.where(lower, dt_f, jnp.where(upper, dt_b, dt_f + dt_b))
            mats.append((gmat[g] * jnp.exp(seg) * dsel).astype(BF16))
        yp = _dot(jnp.concatenate(mats, axis=0), ux_ref[:, cols])
        y = jnp.where(lane < SSD_HEAD_DIM, yp[:SSD_L], yp[SSD_L:])
        g = (2 * p) // HEADS_PER_GROUP
        gc = slice(p * LANES - g * GROUP_DIM, (p + 1) * LANES - g * GROUP_DIM)
        y = y + cross[g][:, gc] * e_full[:, cols]
        y = y + yoffb_ref[:, cols] + dskip_ref[:, cols] * ux_ref[:, cols].astype(F32)
        zf = z_ref[:, cols].astype(F32)
        gated.append(y * (zf * jax.nn.sigmoid(zf)))
    per_group = len(gated) // SSD_GROUPS
    for g in range(SSD_GROUPS):
        mine = gated[g * per_group:(g + 1) * per_group]
        ssq = sum(jnp.sum(v * v, axis=-1, keepdims=True) for v in mine)
        scale = lax.rsqrt(ssq / GROUP_DIM + RMS_EPS)
        for j, v in enumerate(mine):
            cols = slice((g * per_group + j) * LANES, (g * per_group + j + 1) * LANES)
            out_ref[:, cols] = (v * scale * normw_ref[:, cols]).astype(BF16)
    w_t = jnp.exp(cum_t[:, SSD_L - 1:SSD_L] - cum_t) * dt_t
    cd_t = jnp.exp(cum_t[:, SSD_L - 1:SSD_L])
    _update_state(state_scr, ux_ref[...].astype(F32).T, w_t, cd_t, b_cols, 0)


def _head_expand_matrix(row_off):
    h = jnp.arange(LANES)[:, None]
    c = jnp.arange(SSD_DIM)[None, :] // SSD_HEAD_DIM
    return jnp.where(h == c + row_off, 1.0, 0.0).astype(BF16)


def bissd(ux, ubc, dt_raw, proj, dt_bias_row, a_log_row, d_skip_row, norm_w_row, geom):
    t = ux.shape[0]
    n_chunks = t // SSD_L
    cp = pltpu.CompilerParams(dimension_semantics=("arbitrary",), vmem_limit_bytes=VMEM_LIMIT)
    row128 = pl.BlockSpec((1, LANES), lambda i: (0, 0))
    row_dim = pl.BlockSpec((1, SSD_DIM), lambda i: (0, 0))
    expand_spec = pl.BlockSpec((LANES, SSD_DIM), lambda i: (0, 0))
    state = pltpu.VMEM((SSD_DIM, SSD_STATE), F32)

    def rev(width):
        return pl.BlockSpec((SSD_L, width), lambda i: (n_chunks - 1 - i, 0))

    def fwd(width, col_block=0):
        return pl.BlockSpec((SSD_L, width), lambda i: (i, col_block))

    yoffb = pl.pallas_call(
        functools.partial(_ssd_bwd_kernel, geom=geom, n_chunks=n_chunks),
        out_shape=jax.ShapeDtypeStruct((t, SSD_DIM), F32),
        grid=(n_chunks,),
        in_specs=[rev(SSD_DIM), rev(BC_DIM), rev(LANES), row128, row128, expand_spec],
        out_specs=rev(SSD_DIM),
        scratch_shapes=[state],
        compiler_params=cp,
        name="ssd_backward",
    )(ux, ubc, dt_raw, dt_bias_row, a_log_row, _head_expand_matrix(BWD))
    z_col_block = 3 * NA_DIM // SSD_DIM
    return pl.pallas_call(
        functools.partial(_ssd_fwd_kernel, geom=geom),
        out_shape=jax.ShapeDtypeStruct((t, SSD_DIM), BF16),
        grid=(n_chunks,),
        in_specs=[fwd(SSD_DIM), fwd(BC_DIM), fwd(LANES), fwd(SSD_DIM, z_col_block), fwd(SSD_DIM),
                  row128, row128, expand_spec, row_dim, row_dim],
        out_specs=fwd(SSD_DIM),
        scratch_shapes=[state],
        compiler_params=cp,
        name="ssd_forward",
    )(ux, ubc, dt_raw, proj, yoffb, dt_bias_row, a_log_row, _head_expand_matrix(0),
      d_skip_row, norm_w_row)


OUT_TM = 256


def _out_proj_kernel(na_ref, ssd_ref, xp_ref, xs_ref, wna_ref, wssd_ref, nw_ref,
                     rw_hi_ref, rw_lo_ref, rb_ref, x1_ref, h2_ref, logit_ref, *, n_prompt_blocks):
    i = pl.program_id(0)
    mix = _dot(na_ref[...], wna_ref[...]) + _dot(ssd_ref[...], wssd_ref[...])

    @pl.when(i < n_prompt_blocks)
    def _():
        x1_ref[...] = xp_ref[...] + mix

    @pl.when(i >= n_prompt_blocks)
    def _():
        x1_ref[...] = xs_ref[...] + mix

    x1 = x1_ref[...]
    ms = jnp.mean(x1 * x1, axis=-1, keepdims=True)
    h2 = x1 * lax.rsqrt(ms + RMS_EPS) * nw_ref[...]
    hi, lo = _split_hi_lo(h2)
    h2_ref[...] = hi
    rw_hi = rw_hi_ref[...]
    logit_ref[...] = _dot(hi, rw_hi) + _dot(lo, rw_hi) + _dot(hi, rw_lo_ref[...]) + rb_ref[...]


def out_proj(na, ssd, xp, xs, w_na, w_ssd, norm_w, rw_hi, rw_lo, rb):
    t = na.shape[0]
    d = xp.shape[1]
    tm = OUT_TM
    npb = xp.shape[0] // tm
    xp_spec, xs_spec = _two_source_specs(tm, d, npb)
    row = lambda w: pl.BlockSpec((tm, w), lambda i: (i, 0))
    return pl.pallas_call(
        functools.partial(_out_proj_kernel, n_prompt_blocks=npb),
        out_shape=(jax.ShapeDtypeStruct((t, d), F32), jax.ShapeDtypeStruct((t, d), BF16),
                   jax.ShapeDtypeStruct((t, LANES), F32)),
        grid=(t // tm,),
        in_specs=[row(NA_DIM), row(SSD_DIM), xp_spec, xs_spec,
                  _resident((NA_DIM, d)), _resident((SSD_DIM, d)), _resident((1, d)),
                  _resident((d, LANES)), _resident((d, LANES)), _resident((1, LANES))],
        out_specs=(row(d), row(d), row(LANES)),
        compiler_params=pltpu.CompilerParams(dimension_semantics=("arbitrary",),
                                             vmem_limit_bytes=VMEM_LIMIT),
        name="out_proj",
    )(na, ssd, xp, xs, w_na, w_ssd, norm_w, rw_hi, rw_lo, rb)


MOE_BLK = 512
MOE_TF = 512


def _moe_kernel(be_ref, nused_ref, x_ref, wg_ref, bg_ref, wu_ref, bu_ref, wd_ref, bd_ref, sw_ref,
                y_ref, acc_scr, *, n_f):
    b = pl.program_id(0)
    f = pl.program_id(1)
    used = b < nused_ref[0]

    @pl.when(used)
    def _():
        x = x_ref[...]
        g = jnp.minimum(_dot(x, wg_ref[0].astype(BF16)) + bg_ref[0], SWIGLU_LIMIT)
        u = jnp.clip(_dot(x, wu_ref[0].astype(BF16)) + bu_ref[0], -SWIGLU_LIMIT, SWIGLU_LIMIT)
        act = (u + 1.0) * (g * jax.nn.sigmoid(SWIGLU_ALPHA * g))
        part = _dot(act.astype(BF16), wd_ref[0].astype(BF16))

        @pl.when(f == 0)
        def _():
            acc_scr[...] = part

        @pl.when(f > 0)
        def _():
            acc_scr[...] += part

        @pl.when(f == n_f - 1)
        def _():
            y_ref[...] = ((acc_scr[...] + bd_ref[0]) * sw_ref[...]).astype(BF16)

    @pl.when(jnp.logical_and(jnp.logical_not(used), f == n_f - 1))
    def _():
        y_ref[...] = jnp.zeros_like(y_ref)


def moe_experts(block_e, n_used, xs, w_gate, b_gate, w_up, b_up, w_down, b_down, slot_w):
    slots, d = xs.shape
    d_ff = w_gate.shape[2]
    n_blocks = slots // MOE_BLK
    n_f = d_ff // MOE_TF

    def f_idx(b, f, nu):
        return jnp.where(b < nu[0], f, 0)

    grid_spec = pltpu.PrefetchScalarGridSpec(
        num_scalar_prefetch=2,
        grid=(n_blocks, n_f),
        in_specs=[pl.BlockSpec((MOE_BLK, d), lambda b, f, be, nu: (b, 0)),
                  pl.BlockSpec((1, d, MOE_TF), lambda b, f, be, nu: (be[b], 0, f_idx(b, f, nu))),
                  pl.BlockSpec((1, 1, MOE_TF), lambda b, f, be, nu: (be[b], 0, f_idx(b, f, nu))),
                  pl.BlockSpec((1, d, MOE_TF), lambda b, f, be, nu: (be[b], 0, f_idx(b, f, nu))),
                  pl.BlockSpec((1, 1, MOE_TF), lambda b, f, be, nu: (be[b], 0, f_idx(b, f, nu))),
                  pl.BlockSpec((1, MOE_TF, d), lambda b, f, be, nu: (be[b], f_idx(b, f, nu), 0)),
                  pl.BlockSpec((1, 1, d), lambda b, f, be, nu: (be[b], 0, 0)),
                  pl.BlockSpec((MOE_BLK, 1), lambda b, f, be, nu: (b, 0))],
        out_specs=pl.BlockSpec((MOE_BLK, d), lambda b, f, be, nu: (b, 0)),
        scratch_shapes=[pltpu.VMEM((MOE_BLK, d), F32)],
    )
    return pl.pallas_call(
        functools.partial(_moe_kernel, n_f=n_f),
        out_shape=jax.ShapeDtypeStruct((slots, d), BF16),
        grid_spec=grid_spec,
        compiler_params=pltpu.CompilerParams(dimension_semantics=("arbitrary", "arbitrary"),
                                             vmem_limit_bytes=VMEM_LIMIT),
        name="moe_experts",
    )(block_e, n_used, xs, w_gate, b_gate, w_up, b_up, w_down, b_down, slot_w)


FIN_TM = 256


def _final_kernel(x1_ref, *refs):
    y_refs, nw_ref, o_ref = refs[:TOP_K], refs[TOP_K], refs[TOP_K + 1]
    x = x1_ref[...]
    for y_ref in y_refs:
        x = x + y_ref[...].astype(F32)
    ms = jnp.mean(x * x, axis=-1, keepdims=True)
    o_ref[...] = x * lax.rsqrt(ms + RMS_EPS) * nw_ref[...]


def combine_and_norm(x1, yg, norm_w):
    t, d = x1.shape
    tm = FIN_TM
    per_k = t // tm

    def choice(k):
        return pl.BlockSpec((tm, d), lambda i: (k * per_k + i, 0))

    return pl.pallas_call(
        _final_kernel,
        out_shape=jax.ShapeDtypeStruct((t, d), F32),
        grid=(per_k,),
        in_specs=[pl.BlockSpec((tm, d), lambda i: (i, 0))] + [choice(k) for k in range(TOP_K)]
                 + [pl.BlockSpec((1, d), lambda i: (0, 0))],
        out_specs=pl.BlockSpec((tm, d), lambda i: (i, 0)),
        compiler_params=pltpu.CompilerParams(dimension_semantics=("arbitrary",),
                                             vmem_limit_bytes=VMEM_LIMIT),
        name="combine_norm",
    )(x1, *([yg] * TOP_K), norm_w)


def _route(logits):
    t = logits.shape[0]
    m = t * TOP_K
    top_logit, top_e = lax.top_k(logits, TOP_K)
    gate_w = jax.nn.softmax(top_logit, axis=-1)
    flat_e = top_e.reshape(m).astype(jnp.int32)
    iota = jnp.arange(m, dtype=jnp.int32)
    e_sorted, order, w_sorted = lax.sort((flat_e, iota, gate_w.reshape(m)), num_keys=1, is_stable=True)
    counts = jnp.bincount(flat_e, length=N_EXPERTS).astype(jnp.int32)
    padded = (counts + MOE_BLK - 1) // MOE_BLK * MOE_BLK
    pad_end = jnp.cumsum(padded)
    pad_start = pad_end - padded
    grp_start = jnp.cumsum(counts) - counts
    dest = pad_start[e_sorted] + iota - grp_start[e_sorted]
    n_blocks = -(-m // MOE_BLK) + N_EXPERTS
    block_e = jnp.minimum(jnp.searchsorted(pad_end, jnp.arange(n_blocks, dtype=jnp.int32) * MOE_BLK,
                                           side='right'), N_EXPERTS - 1).astype(jnp.int32)
    n_used = (pad_end[-1:] // MOE_BLK).astype(jnp.int32)
    e_slot = jnp.repeat(block_e, MOE_BLK)
    off = jnp.arange(n_blocks * MOE_BLK, dtype=jnp.int32) - pad_start[e_slot]
    valid = off < counts[e_slot]
    src = jnp.clip(grp_start[e_slot] + off, 0, m - 1)
    slot_tok = jnp.where(valid, order[src] // TOP_K, 0)
    slot_w = jnp.where(valid, w_sorted[src], 0.0)
    _, pos = lax.sort((order, dest), num_keys=1)
    pos = pos.reshape(t, TOP_K).T.reshape(m)
    return slot_tok, slot_w, block_e, n_used, pos


def kernel(x_prompt, x_sample, norm_mix_w, w_in, na_rpb, conv_w, conv_b, dt_bias_fwd, dt_bias_bwd,
           a_log_fwd, a_log_bwd, d_skip, ssd_norm_w, w_out, norm_ffn_w, router_w, router_b,
           w_gate, b_gate, w_up, b_up, w_down, b_down, final_norm_w):
    b1, s1, d = x_prompt.shape
    b2, s2, _ = x_sample.shape
    geom = (b1, s1, b2, s2)
    for s in (s1, s2):
        assert s % (NA_ROWS * GRID_W) == 0 and s % CONV_TL == 0 and s % SSD_L == 0
    assert w_in.shape[0] == 1, "one layer"
    xp = x_prompt.reshape(b1 * s1, d)
    xs = x_sample.reshape(b2 * s2, d)
    t = xp.shape[0] + xs.shape[0]

    def pad_lanes(a):
        return jnp.pad(a, ((0, 0), (0, LANES - a.shape[1])))

    wi = w_in[0]
    wdt_hi, wdt_lo = _split_hi_lo(pad_lanes(wi[:, PROJ_DIM:]))
    proj, dt_raw = in_proj(xp, xs, norm_mix_w, wi[:, :PROJ_DIM].astype(BF16), wdt_hi, wdt_lo)

    na = neighbourhood_attention(proj, na_bias_table(na_rpb[0]), geom)

    ux, ubc = conv_silu(proj, jnp.pad(conv_w[0], ((0, 8 - CONV_K), (0, 0))), conv_b, geom)
    dt_bias_row = pad_lanes(jnp.concatenate([dt_bias_fwd, dt_bias_bwd], axis=1))
    a_log_row = pad_lanes(jnp.concatenate([a_log_fwd, a_log_bwd], axis=1))
    d_skip_row = jnp.repeat(d_skip, SSD_HEAD_DIM, axis=1)
    ssd = bissd(ux, ubc, dt_raw, proj, dt_bias_row, a_log_row, d_skip_row, ssd_norm_w, geom)

    wo = w_out[0].astype(BF16)
    rw_hi, rw_lo = _split_hi_lo(pad_lanes(router_w[0]))
    x1, h2, logits = out_proj(na, ssd, xp, xs, wo[:NA_DIM], wo[NA_DIM:], norm_ffn_w,
                              rw_hi, rw_lo, pad_lanes(router_b))

    slot_tok, slot_w, block_e, n_used, pos = _route(logits[:, :N_EXPERTS])
    y = moe_experts(block_e, n_used, h2[slot_tok], w_gate[0], b_gate[0][:, None, :],
                    w_up[0], b_up[0][:, None, :], w_down[0], b_down[0][:, None, :], slot_w[:, None])
    out = combine_and_norm(x1, y[pos], final_norm_w[None, :])
    return (out[:b1 * s1].reshape(b1, s1, d), out[b1 * s1:].reshape(b2, s2, d))
```
